```python
import jax, jax.numpy as jnp
from jax import lax
import numpy as np

D_MODEL = 1024
BATCH = 16
SEQ = 2048
DEPTH = 4
DEC_BATCH = 128
DEC_SEQ = 4
PAST_LEN = 8192
PAGE_SIZE = 128

N_EVEN = (DEPTH + 1) // 2
N_ODD = DEPTH // 2
D_POOL = D_MODEL // 2
POOL_WINDOWS = (2, 4, 8, 16)
POOL_GROUPS = len(POOL_WINDOWS)
POOL_CH = D_POOL // POOL_GROUPS
POOL_BUF = max(POOL_WINDOWS) - 1
D_SCONV = D_MODEL // 2
SCONV_W = 3
D_IN_EVEN = D_POOL + 3 * D_SCONV
MLA_HEADS = 16
QK_NOPE = 128
QK_ROPE = 64
V_HEAD = 128
Q_LORA = 384
KV_LORA = 256
D_IN_ODD = Q_LORA + KV_LORA + QK_ROPE
ROPE_THETA = 10000.0
ATTN_SCALE = (QK_NOPE + QK_ROPE) ** -0.5
Q_BLOCK = 128
D_FF = 2816
FFN_CONV_W = 3
NORM_EPS = 1e-6

kernel_name = 'hybrid_pool_sconv_mla_convffn_step'


def rmsnorm(x, g):
    xf = x.astype(jnp.float32)
    inv = lax.rsqrt(jnp.mean(xf * xf, axis=-1, keepdims=True) + NORM_EPS)
    return (xf * inv * g.astype(jnp.float32)).astype(x.dtype)


def rope(x, pos):
    half = x.shape[-1] // 2
    freqs = ROPE_THETA ** (-jnp.arange(half, dtype=jnp.float32) / half)
    ang = pos.astype(jnp.float32)[:, None] * freqs[None, :]
    shape = (pos.shape[0],) + (1,) * (x.ndim - 3) + (half,)
    cos = jnp.cos(ang).reshape(shape)
    sin = jnp.sin(ang).reshape(shape)
    xf = x.astype(jnp.float32)
    x1, x2 = xf[..., :half], xf[..., half:]
    return jnp.concatenate([x1 * cos - x2 * sin, x2 * cos + x1 * sin], axis=-1).astype(x.dtype)


def multiscale_pool(u_ext, n_new):
    L = u_ext.shape[1]
    uf = u_ext.astype(jnp.float32)
    cs0 = jnp.pad(jnp.cumsum(uf, axis=1), ((0, 0), (1, 0), (0, 0), (0, 0)))
    row = jnp.arange(L - n_new, L)
    outs = []
    for g, w in enumerate(POOL_WINDOWS):
        csg = jnp.pad(cs0[:, :, g], ((0, 0), (w, 0), (0, 0)))
        total = csg[:, L - n_new + 1 + w: L + 1 + w] - csg[:, L - n_new + 1: L + 1]
        count = jnp.minimum(row + 1, w).astype(jnp.float32)
        outs.append(total / count[None, :, None] - uf[:, L - n_new:, g])
    return jnp.stack(outs, axis=2).astype(u_ext.dtype)


def causal_dwconv(v_ext, w, n_new):
    K = w.shape[0]
    off = v_ext.shape[1] - n_new - (K - 1)
    out = w[0] * v_ext[:, off: off + n_new]
    for k in range(1, K):
        out = out + w[k] * v_ext[:, off + k: off + k + n_new]
    return out


def even_mixer(xn, w_in, w_map, scale, w_conv, w_out, pool_prev, conv_prev):
    b, t, _ = xn.shape
    h = xn @ w_in
    u = h[..., :D_POOL]
    gate_b, gate_c, hx = jnp.split(h[..., D_POOL:], 3, axis=-1)
    u_ext = jnp.concatenate([pool_prev, u], axis=1)
    d = multiscale_pool(u_ext.reshape(b, -1, POOL_GROUPS, POOL_CH), t)
    y_pool = jnp.einsum('btgc,gcd->btgd', d, w_map).reshape(b, t, D_POOL) * scale
    v_ext = jnp.concatenate([conv_prev, gate_c * hx], axis=1)
    y_conv = gate_b * causal_dwconv(v_ext, w_conv, t)
    y = jnp.concatenate([y_pool, y_conv], axis=-1) @ w_out
    return y, u_ext[:, -POOL_BUF:], v_ext[:, -(SCONV_W - 1):]


def latent_attend(q_lat, q_rope, k_lat, k_rope, q_pos, k_pos):
    s = (jnp.einsum('bqhr,bkr->bhqk', q_lat, k_lat)
         + jnp.einsum('bqhp,bkp->bhqk', q_rope, k_rope)).astype(jnp.float32) * ATTN_SCALE
    mask = (k_pos[None, :] <= q_pos[:, None])[None, None]
    s = jnp.where(mask, s, jnp.finfo(jnp.float32).min)
    p = jax.nn.softmax(s, axis=-1).astype(k_lat.dtype)
    return jnp.einsum('bhqk,bkr->bqhr', p, k_lat)


def mla_mixer(xn, past_ckv, past_kr, w_in, q_norm, kv_norm, w_uq, w_uk, w_uv, w_out):
    b, t, _ = xn.shape
    p_len = past_ckv.shape[1]
    q_pos = p_len + jnp.arange(t)
    k_pos = jnp.arange(p_len + t)
    h = xn @ w_in
    c_q = rmsnorm(h[..., :Q_LORA], q_norm)
    c_kv = rmsnorm(h[..., Q_LORA:Q_LORA + KV_LORA], kv_norm)
    k_rope = rope(h[..., Q_LORA + KV_LORA:], q_pos)
    q = jnp.einsum('btr,rhd->bthd', c_q, w_uq)
    q_rope = rope(q[..., QK_NOPE:], q_pos)
    q_lat = jnp.einsum('bthd,rhd->bthr', q[..., :QK_NOPE], w_uk)
    keys_c = jnp.concatenate([past_ckv, c_kv], axis=1)
    keys_r = jnp.concatenate([past_kr, k_rope], axis=1)
    qb = Q_BLOCK if t % Q_BLOCK == 0 else t
    nb = t // qb

    def block(args):
        ql, qr, qp = args
        return latent_attend(ql, qr, keys_c, keys_r, qp, k_pos)

    blocks = (q_lat.reshape(b, nb, qb, MLA_HEADS, KV_LORA).swapaxes(0, 1),
              q_rope.reshape(b, nb, qb, MLA_HEADS, QK_ROPE).swapaxes(0, 1),
              q_pos.reshape(nb, qb))
    o_lat = lax.map(block, blocks).swapaxes(0, 1).reshape(b, t, MLA_HEADS, KV_LORA)
    o = jnp.einsum('bthr,rhv->bthv', o_lat, w_uv).reshape(b, t, MLA_HEADS * V_HEAD)
    return o @ w_out, c_kv, k_rope


def conv_ffn(xn, w_up, w_conv, w_down, prev):
    t = xn.shape[1]
    gate, up = jnp.split(xn @ w_up, 2, axis=-1)
    g_ext = jnp.concatenate([prev, gate], axis=1)
    act = jax.nn.silu(causal_dwconv(g_ext, w_conv, t))
    return (act * up) @ w_down, g_ext[:, -(FFN_CONV_W - 1):]


def run_trunk(x, pool_prev, conv_prev, past_ckv, past_kr, ffn_prev, prm):
    new_pool, new_conv, new_ckv, new_kr, new_ffn = [], [], [], [], []
    for i in range(DEPTH):
        g = prm['norms'][i]
        xn = rmsnorm(x, g[0])
        if i % 2 == 0:
            e = i // 2
            y, sp, sc = even_mixer(xn, prm['w_in_even'][e], prm['w_pool_map'][e], prm['pool_scale'][e],
                                   prm['w_sconv'][e], prm['w_out_even'][e], pool_prev[e], conv_prev[e])
            new_pool.append(sp)
            new_conv.append(sc)
        else:
            o = i // 2
            y, ck, kr = mla_mixer(xn, past_ckv[o], past_kr[o], prm['w_in_odd'][o], prm['q_norm'][o],
                                  prm['kv_norm'][o], prm['w_uq'][o], prm['w_uk'][o], prm['w_uv'][o],
                                  prm['w_out_odd'][o])
            new_ckv.append(ck)
            new_kr.append(kr)
        x = x + rmsnorm(y, g[1])
        y, sf = conv_ffn(rmsnorm(x, g[2]), prm['w_ffn_up'][i], prm['w_ffn_conv'][i], prm['w_ffn_down'][i], ffn_prev[i])
        new_ffn.append(sf)
        x = x + rmsnorm(y, g[3])
    return (x, jnp.stack(new_pool), jnp.stack(new_conv), jnp.stack(new_ckv), jnp.stack(new_kr), jnp.stack(new_ffn))


def setup_inputs(seed: int = 0) -> dict:
    key = jax.random.key(seed)
    ks = jax.random.split(key, 32)
    f32 = jnp.float32
    n_pages = PAST_LEN // PAGE_SIZE
    n_pool = (5 * DEC_BATCH * n_pages) // 4

    def nrm(k, shape, scale=1.0):
        return jax.random.normal(k, shape, f32) * scale

    page_table = jax.random.permutation(ks[7], n_pool)[: DEC_BATCH * n_pages].reshape(DEC_BATCH, n_pages).astype(jnp.int32)
    return {
        'x_prompt': nrm(ks[0], (BATCH, SEQ, D_MODEL)),
        'x_sample': nrm(ks[1], (DEC_BATCH, DEC_SEQ, D_MODEL)),
        'state_pool': nrm(ks[2], (N_EVEN, DEC_BATCH, POOL_BUF, D_POOL)),
        'state_sconv': nrm(ks[3], (N_EVEN, DEC_BATCH, SCONV_W - 1, D_SCONV)),
        'cache_ckv': nrm(ks[4], (N_ODD, n_pool, PAGE_SIZE, KV_LORA)),
        'cache_krope': nrm(ks[5], (N_ODD, n_pool, PAGE_SIZE, QK_ROPE)),
        'state_ffn': nrm(ks[6], (DEPTH, DEC_BATCH, FFN_CONV_W - 1, D_FF)),
        'page_table': page_table,
        'norms': 1.0 + nrm(ks[8], (DEPTH, 4, D_MODEL), 0.1),
        'w_in_even': nrm(ks[9], (N_EVEN, D_MODEL, D_IN_EVEN), D_MODEL ** -0.5),
        'w_pool_map': nrm(ks[10], (N_EVEN, POOL_GROUPS, POOL_CH, POOL_CH), POOL_CH ** -0.5),
        'pool_scale': 1.0 + nrm(ks[11], (N_EVEN, D_POOL), 0.1),
        'w_sconv': nrm(ks[12], (N_EVEN, SCONV_W, D_SCONV), SCONV_W ** -0.5),
        'w_out_even': nrm(ks[13], (N_EVEN, D_POOL + D_SCONV, D_MODEL), (D_POOL + D_SCONV) ** -0.5),
        'w_in_odd': nrm(ks[14], (N_ODD, D_MODEL, D_IN_ODD), D_MODEL ** -0.5),
        'q_norm': 1.0 + nrm(ks[15], (N_ODD, Q_LORA), 0.1),
        'kv_norm': 1.0 + nrm(ks[16], (N_ODD, KV_LORA), 0.1),
        'w_uq': nrm(ks[17], (N_ODD, Q_LORA, MLA_HEADS, QK_NOPE + QK_ROPE), Q_LORA ** -0.5),
        'w_uk': nrm(ks[18], (N_ODD, KV_LORA, MLA_HEADS, QK_NOPE), KV_LORA ** -0.5),
        'w_uv': nrm(ks[19], (N_ODD, KV_LORA, MLA_HEADS, V_HEAD), KV_LORA ** -0.5),
        'w_out_odd': nrm(ks[20], (N_ODD, MLA_HEADS * V_HEAD, D_MODEL), (MLA_HEADS * V_HEAD) ** -0.5),
        'w_ffn_up': nrm(ks[21], (DEPTH, D_MODEL, 2 * D_FF), D_MODEL ** -0.5),
        'w_ffn_conv': nrm(ks[22], (DEPTH, FFN_CONV_W, D_FF), FFN_CONV_W ** -0.5),
        'w_ffn_down': nrm(ks[23], (DEPTH, D_FF, D_MODEL), D_FF ** -0.5),
    }


def reference(x_prompt, x_sample, state_pool, state_sconv, cache_ckv, cache_krope, state_ffn, page_table,
              norms, w_in_even, w_pool_map, pool_scale, w_sconv, w_out_even, w_in_odd, q_norm, kv_norm,
              w_uq, w_uk, w_uv, w_out_odd, w_ffn_up, w_ffn_conv, w_ffn_down):
    prm = {'norms': norms, 'w_in_even': w_in_even, 'w_pool_map': w_pool_map, 'pool_scale': pool_scale,
           'w_sconv': w_sconv, 'w_out_even': w_out_even, 'w_in_odd': w_in_odd, 'q_norm': q_norm,
           'kv_norm': kv_norm, 'w_uq': w_uq, 'w_uk': w_uk, 'w_uv': w_uv, 'w_out_odd': w_out_odd,
           'w_ffn_up': w_ffn_up, 'w_ffn_conv': w_ffn_conv, 'w_ffn_down': w_ffn_down}
    dt = x_prompt.dtype
    bp = x_prompt.shape[0]
    y_prompt, pool_p, sconv_p, ckv_p, kr_p, ffn_p = run_trunk(
        x_prompt,
        [jnp.zeros((bp, 0, D_POOL), dt)] * N_EVEN,
        [jnp.zeros((bp, SCONV_W - 1, D_SCONV), dt)] * N_EVEN,
        [jnp.zeros((bp, 0, KV_LORA), dt)] * N_ODD,
        [jnp.zeros((bp, 0, QK_ROPE), dt)] * N_ODD,
        [jnp.zeros((bp, FFN_CONV_W - 1, D_FF), dt)] * DEPTH,
        prm)
    db = x_sample.shape[0]
    past_len = page_table.shape[1] * cache_ckv.shape[2]
    past_ckv = [cache_ckv[o][page_table].reshape(db, past_len, KV_LORA) for o in range(N_ODD)]
    past_kr = [cache_krope[o][page_table].reshape(db, past_len, QK_ROPE) for o in range(N_ODD)]
    y_sample, pool_s, sconv_s, ckv_s, kr_s, ffn_s = run_trunk(
        x_sample,
        [state_pool[e] for e in range(N_EVEN)],
        [state_sconv[e] for e in range(N_EVEN)],
        past_ckv, past_kr,
        [state_ffn[i] for i in range(DEPTH)],
        prm)
    return (y_prompt, y_sample, pool_p, pool_s, sconv_p, sconv_s, ckv_p, kr_p, ckv_s, kr_s, ffn_p, ffn_s)
```

```python
import functools

import jax
import jax.numpy as jnp
from jax import lax
from jax.experimental import pallas as pl
from jax.experimental.pallas import tpu as pltpu

F32 = jnp.float32
BF16 = jnp.bfloat16

NORM_EPS = 1e-6
POOL_WINDOWS = (2, 4, 8, 16)
POOL_BUF = max(POOL_WINDOWS) - 1
ROPE_THETA = 10000.0
LANES = 128
SUBLANES = 8
NEG_BIG = -1e30

ROW_TILE = 512
PROJ_ROW_TILE = 256
ATTN_TILE = 512
FFN_CHUNKS = 2
MAX_PAGES_PER_STEP = 16
VMEM_LIMIT = 56 * 1024 * 1024


def _cparams(n_axes):
    return pltpu.CompilerParams(dimension_semantics=("arbitrary",) * n_axes, vmem_limit_bytes=VMEM_LIMIT)


def _const_spec(shape):
    n = len(shape)
    return pl.BlockSpec(shape, lambda *_: (0,) * n, pipeline_mode=pl.Buffered(1))


def _full_spec(shape):
    n = len(shape)
    return pl.BlockSpec(shape, lambda *_: (0,) * n)


def _rms(x, g):
    inv = lax.rsqrt(jnp.mean(x * x, axis=-1, keepdims=True) + NORM_EPS)
    return x * inv * g


def _dot(a, b):
    return jnp.dot(a, b, preferred_element_type=F32)


def _dot_nt(a, b):
    return lax.dot_general(a, b, (((1,), (1,)), ((), ())), preferred_element_type=F32)


def _even_kernel(*refs, rows, stride, halo_u, halo_v, prompt):
    if prompt:
        (x_ref, g_ref, win_ref, wmap_ref, scale_ref, wconv_ref, wout_ref,
         xo_ref, pool_ref, conv_ref, ubuf, vbuf) = refs
    else:
        (x_ref, g_ref, win_ref, wmap_ref, scale_ref, wconv_ref, wout_ref, pool_prev_ref, conv_prev_ref,
         xo_ref, pool_ref, conv_ref, ubuf, vbuf) = refs
    d_pool = ubuf.shape[1]
    pool_ch = d_pool // len(POOL_WINDOWS)

    if prompt:
        j = pl.program_id(1)

        @pl.when(j == 0)
        def _():
            ubuf[0:halo_u, :] = jnp.zeros((halo_u, d_pool), F32)
            vbuf[0:halo_v, :] = jnp.zeros((halo_v, d_pool), F32)
    else:
        ubuf[0:halo_u, :] = pool_prev_ref[...]
        vbuf[0:halo_v, :] = conv_prev_ref[...]

    x = x_ref[...]
    xn = _rms(x, g_ref[0:1, :])
    h = _dot(xn.astype(BF16), win_ref[...])
    u = h[:, 0:d_pool]
    gate_b = h[:, d_pool:2 * d_pool]
    gate_c = h[:, 2 * d_pool:3 * d_pool]
    hx = h[:, 3 * d_pool:4 * d_pool]
    ubuf[halo_u:halo_u + rows, :] = u
    vbuf[halo_v:halo_v + rows, :] = gate_c * hx

    if prompt:
        pos = j * rows + lax.broadcasted_iota(jnp.int32, (rows, 1), 0)
    pieces = []
    for g, w in enumerate(POOL_WINDOWS):
        cols = slice(g * pool_ch, (g + 1) * pool_ch)
        cur = ubuf[halo_u:halo_u + rows, cols]
        total = cur
        for s in range(1, w):
            total = total + ubuf[halo_u - s * stride:halo_u - s * stride + rows, cols]
        if prompt:
            count = jnp.minimum(pos + 1, w).astype(F32)
            pieces.append(total / count - cur)
        else:
            pieces.append(total / float(w) - cur)
    d = jnp.concatenate(pieces, axis=1)
    y_pool = _dot(d.astype(BF16), wmap_ref[...]) * scale_ref[...]

    conv = (wconv_ref[0:1, :] * vbuf[halo_v - 2 * stride:halo_v - 2 * stride + rows, :]
            + wconv_ref[1:2, :] * vbuf[halo_v - stride:halo_v - stride + rows, :]
            + wconv_ref[2:3, :] * vbuf[halo_v:halo_v + rows, :])
    y_conv = gate_b * conv
    y = _dot(jnp.concatenate([y_pool, y_conv], axis=1).astype(BF16), wout_ref[...])
    xo_ref[...] = x + _rms(y, g_ref[1:2, :])

    new_pool = ubuf[halo_u + rows - POOL_BUF * stride:halo_u + rows, :]
    new_conv = vbuf[halo_v + rows - 2 * stride:halo_v + rows, :]
    if prompt:
        pool_ref[0] = new_pool
        conv_ref[0] = new_conv
        ubuf[0:halo_u, :] = ubuf[rows:rows + halo_u, :]
        vbuf[0:halo_v, :] = vbuf[rows:rows + halo_v, :]
    else:
        pool_ref[...] = new_pool
        conv_ref[...] = new_conv


def _even_prompt(x2d, batch, seq, g, win, wmap, scale, wconv, wout):
    d_model = x2d.shape[1]
    d_pool = wmap.shape[0]
    rows = min(ROW_TILE, seq)
    n_t = seq // rows
    halo_u, halo_v = 2 * SUBLANES, SUBLANES
    row_spec = pl.BlockSpec((rows, d_model), lambda b, j: (b * n_t + j, 0))
    return pl.pallas_call(
        functools.partial(_even_kernel, rows=rows, stride=1, halo_u=halo_u, halo_v=halo_v, prompt=True),
        grid=(batch, n_t),
        in_specs=[row_spec, _const_spec(g.shape), _const_spec(win.shape), _const_spec(wmap.shape),
                  _const_spec(scale.shape), _const_spec(wconv.shape), _const_spec(wout.shape)],
        out_specs=[row_spec,
                   pl.BlockSpec((1, POOL_BUF, d_pool), lambda b, j: (b, 0, 0)),
                   pl.BlockSpec((1, 2, d_pool), lambda b, j: (b, 0, 0))],
        out_shape=[jax.ShapeDtypeStruct(x2d.shape, F32),
                   jax.ShapeDtypeStruct((batch, POOL_BUF, d_pool), F32),
                   jax.ShapeDtypeStruct((batch, 2, d_pool), F32)],
        scratch_shapes=[pltpu.VMEM((halo_u + rows, d_pool), F32), pltpu.VMEM((halo_v + rows, d_pool), F32)],
        compiler_params=_cparams(2),
        name="even_prompt",
    )(x2d, g, win, wmap, scale, wconv, wout)


def _even_sample(x2d, dec_batch, g, win, wmap, scale, wconv, wout, pool_prev, conv_prev):
    rows, d_model = x2d.shape
    d_pool = wmap.shape[0]
    halo_u, halo_v = POOL_BUF * dec_batch, 2 * dec_batch
    args = (x2d, g, win, wmap, scale, wconv, wout, pool_prev, conv_prev)
    return pl.pallas_call(
        functools.partial(_even_kernel, rows=rows, stride=dec_batch, halo_u=halo_u, halo_v=halo_v,
                          prompt=False),
        grid=(1,),
        in_specs=[_const_spec(a.shape) for a in args],
        out_specs=[_full_spec(x2d.shape), _full_spec(pool_prev.shape), _full_spec(conv_prev.shape)],
        out_shape=[jax.ShapeDtypeStruct(x2d.shape, F32),
                   jax.ShapeDtypeStruct(pool_prev.shape, F32),
                   jax.ShapeDtypeStruct(conv_prev.shape, F32)],
        scratch_shapes=[pltpu.VMEM((halo_u + rows, d_pool), F32), pltpu.VMEM((halo_v + rows, d_pool), F32)],
        compiler_params=_cparams(1),
        name="even_sample",
    )(*args)


def _ffn_kernel(*refs, rows, stride, halo, prompt):
    if prompt:
        x_ref, g_ref, wup_ref, wconv_ref, wdown_ref, xo_ref, state_ref, gbuf = refs
    else:
        x_ref, g_ref, wup_ref, wconv_ref, wdown_ref, prev_ref, xo_ref, state_ref, gbuf = refs
    d_ff = gbuf.shape[1]
    chunk = d_ff // FFN_CHUNKS

    if prompt:
        @pl.when(pl.program_id(1) == 0)
        def _():
            gbuf[0:halo, :] = jnp.zeros((halo, d_ff), F32)
    else:
        gbuf[0:halo, :] = prev_ref[...]

    x = x_ref[...]
    xn = _rms(x, g_ref[0:1, :]).astype(BF16)
    y = None
    for c in range(FFN_CHUNKS):
        cols = slice(c * chunk, (c + 1) * chunk)
        gate = _dot(xn, wup_ref[:, c * chunk:(c + 1) * chunk])
        up = _dot(xn, wup_ref[:, d_ff + c * chunk:d_ff + (c + 1) * chunk])
        gbuf[halo:halo + rows, cols] = gate
        conv = (wconv_ref[0:1, cols] * gbuf[halo - 2 * stride:halo - 2 * stride + rows, cols]
                + wconv_ref[1:2, cols] * gbuf[halo - stride:halo - stride + rows, cols]
                + wconv_ref[2:3, cols] * gate)
        act = conv * jax.nn.sigmoid(conv) * up
        part = _dot(act.astype(BF16), wdown_ref[c * chunk:(c + 1) * chunk, :])
        y = part if y is None else y + part
    xo_ref[...] = x + _rms(y, g_ref[1:2, :])

    new_state = gbuf[halo + rows - 2 * stride:halo + rows, :]
    if prompt:
        state_ref[0] = new_state
        gbuf[0:halo, :] = gbuf[rows:rows + halo, :]
    else:
        state_ref[...] = new_state


def _ffn_prompt(x2d, batch, seq, g, wup, wconv, wdown):
    d_model = x2d.shape[1]
    d_ff = wdown.shape[0]
    rows = min(ROW_TILE, seq)
    n_t = seq // rows
    halo = SUBLANES
    row_spec = pl.BlockSpec((rows, d_model), lambda b, j: (b * n_t + j, 0))
    return pl.pallas_call(
        functools.partial(_ffn_kernel, rows=rows, stride=1, halo=halo, prompt=True),
        grid=(batch, n_t),
        in_specs=[row_spec, _const_spec(g.shape), _const_spec(wup.shape), _const_spec(wconv.shape),
                  _const_spec(wdown.shape)],
        out_specs=[row_spec, pl.BlockSpec((1, 2, d_ff), lambda b, j: (b, 0, 0))],
        out_shape=[jax.ShapeDtypeStruct(x2d.shape, F32), jax.ShapeDtypeStruct((batch, 2, d_ff), F32)],
        scratch_shapes=[pltpu.VMEM((halo + rows, d_ff), F32)],
        compiler_params=_cparams(2),
        name="ffn_prompt",
    )(x2d, g, wup, wconv, wdown)


def _ffn_sample(x2d, dec_batch, g, wup, wconv, wdown, prev):
    rows = x2d.shape[0]
    d_ff = wdown.shape[0]
    halo = 2 * dec_batch
    args = (x2d, g, wup, wconv, wdown, prev)
    return pl.pallas_call(
        functools.partial(_ffn_kernel, rows=rows, stride=dec_batch, halo=halo, prompt=False),
        grid=(1,),
        in_specs=[_const_spec(a.shape) for a in args],
        out_specs=[_full_spec(x2d.shape), _full_spec(prev.shape)],
        out_shape=[jax.ShapeDtypeStruct(x2d.shape, F32), jax.ShapeDtypeStruct(prev.shape, F32)],
        scratch_shapes=[pltpu.VMEM((halo + rows, d_ff), F32)],
        compiler_params=_cparams(1),
        name="ffn_sample",
    )(*args)


def _mla_proj_kernel(*refs, heads, q_lora, kv_lora, qk_nope, absorbed, attn_scale):
    if absorbed:
        (x_ref, g_ref, win_ref, qn_ref, kvn_ref, wq_ref, wukt_ref, cos_ref, sin_ref,
         ckv_ref, kr_ref, qlat_ref, qrot_ref) = refs
    else:
        (x_ref, g_ref, win_ref, qn_ref, kvn_ref, wq_ref, wk_ref, wv_ref, cos_ref, sin_ref,
         ckv_ref, kr_ref, q_ref, k_ref, v_ref) = refs
    rope = kr_ref.shape[-1]
    cos = cos_ref[...]
    sin = sin_ref[...]

    xn = _rms(x_ref[...], g_ref[0:1, :])
    h = _dot(xn.astype(BF16), win_ref[...])
    c_q = _rms(h[:, 0:q_lora], qn_ref[...]) * attn_scale
    c_kv = _rms(h[:, q_lora:q_lora + kv_lora], kvn_ref[...])
    base = q_lora + kv_lora
    kr_lo = h[:, base:base + LANES] * cos + h[:, base + LANES:base + 2 * LANES] * sin
    ckv_ref[...] = c_kv
    kr_ref[...] = kr_lo[:, 0:rope]

    q_all = _dot(c_q.astype(BF16), wq_ref[...])
    n_nope = heads * qk_nope
    n_rope = heads * rope
    q_rot = [q_all[:, n_nope + p * LANES:n_nope + (p + 1) * LANES] * cos
             + q_all[:, n_nope + n_rope + p * LANES:n_nope + n_rope + (p + 1) * LANES] * sin
             for p in range(n_rope // LANES)]

    if absorbed:
        for hd in range(heads):
            q_nope = q_all[:, hd * qk_nope:(hd + 1) * qk_nope].astype(BF16)
            qlat_ref[hd] = _dot(q_nope, wukt_ref[hd]).astype(BF16)
        qrot_ref[...] = jnp.concatenate(q_rot, axis=1).astype(BF16)
    else:
        kr_hi = pltpu.roll(kr_lo, LANES // 2, axis=1)
        ckv_b = c_kv.astype(BF16)
        k_nope = _dot(ckv_b, wk_ref[...])
        v_all = _dot(ckv_b, wv_ref[...])
        v_head = v_ref.shape[-1]
        per_pair = LANES // rope
        for hd in range(heads):
            q_ref[0, hd] = jnp.concatenate(
                [q_all[:, hd * qk_nope:(hd + 1) * qk_nope], q_rot[hd // per_pair]], axis=1).astype(BF16)
            k_ref[0, hd] = jnp.concatenate(
                [k_nope[:, hd * qk_nope:(hd + 1) * qk_nope], kr_lo if hd % per_pair == 0 else kr_hi],
                axis=1).astype(BF16)
            v_ref[0, hd] = v_all[:, hd * v_head:(hd + 1) * v_head].astype(BF16)


def _mla_proj_prompt(x2d, batch, seq, g, win, qn, kvn, wq, wk, wv, cos, sin, *, heads, rope, attn_scale):
    d_model = x2d.shape[1]
    q_lora, kv_lora = qn.shape[1], kvn.shape[1]
    qk_nope = wk.shape[1] // heads
    v_head = wv.shape[1] // heads
    rows = min(PROJ_ROW_TILE, seq)
    n_t = seq // rows
    row_map = lambda b, j: (b * n_t + j, 0)
    head_map = lambda b, j: (b, 0, j, 0)
    return pl.pallas_call(
        functools.partial(_mla_proj_kernel, heads=heads, q_lora=q_lora, kv_lora=kv_lora, qk_nope=qk_nope,
                          absorbed=False, attn_scale=attn_scale),
        grid=(batch, n_t),
        in_specs=[pl.BlockSpec((rows, d_model), row_map), _const_spec(g.shape), _const_spec(win.shape),
                  _const_spec(qn.shape), _const_spec(kvn.shape), _const_spec(wq.shape),
                  _const_spec(wk.shape), _const_spec(wv.shape),
                  pl.BlockSpec((rows, LANES), lambda b, j: (j, 0)),
                  pl.BlockSpec((rows, LANES), lambda b, j: (j, 0))],
        out_specs=[pl.BlockSpec((rows, kv_lora), row_map), pl.BlockSpec((rows, rope), row_map),
                   pl.BlockSpec((1, heads, rows, qk_nope + LANES), head_map),
                   pl.BlockSpec((1, heads, rows, qk_nope + LANES), head_map),
                   pl.BlockSpec((1, heads, rows, v_head), head_map)],
        out_shape=[jax.ShapeDtypeStruct((batch * seq, kv_lora), F32),
                   jax.ShapeDtypeStruct((batch * seq, rope), F32),
                   jax.ShapeDtypeStruct((batch, heads, seq, qk_nope + LANES), BF16),
                   jax.ShapeDtypeStruct((batch, heads, seq, qk_nope + LANES), BF16),
                   jax.ShapeDtypeStruct((batch, heads, seq, v_head), BF16)],
        compiler_params=_cparams(2),
        name="mla_proj_prompt",
    )(x2d, g, win, qn, kvn, wq, wk, wv, cos, sin)


def _mla_proj_sample(x2d, g, win, qn, kvn, wq, wukt, cos, sin, *, heads, rope, attn_scale):
    rows = x2d.shape[0]
    q_lora, kv_lora = qn.shape[1], kvn.shape[1]
    qk_nope = wukt.shape[1]
    args = (x2d, g, win, qn, kvn, wq, wukt, cos, sin)
    out_shapes = [(rows, kv_lora), (rows, rope), (heads, rows, kv_lora), (rows, heads * rope)]
    out_dtypes = [F32, F32, BF16, BF16]
    return pl.pallas_call(
        functools.partial(_mla_proj_kernel, heads=heads, q_lora=q_lora, kv_lora=kv_lora, qk_nope=qk_nope,
                          absorbed=True, attn_scale=attn_scale),
        grid=(1,),
        in_specs=[_const_spec(a.shape) for a in args],
        out_specs=[_full_spec(s) for s in out_shapes],
        out_shape=[jax.ShapeDtypeStruct(s, d) for s, d in zip(out_shapes, out_dtypes)],
        compiler_params=_cparams(1),
        name="mla_proj_sample",
    )(*args)


def _attn_prompt_kernel(q_ref, k_ref, v_ref, o_ref, *, seq, tile):
    n_t = seq // tile
    row = lax.broadcasted_iota(jnp.int32, (tile, tile), 0)
    col = lax.broadcasted_iota(jnp.int32, (tile, tile), 1)
    causal = col <= row
    for qi in range(n_t):
        q = q_ref[0, 0, qi * tile:(qi + 1) * tile, :]
        m = l = acc = None
        for ki in range(qi + 1):
            k = k_ref[0, 0, ki * tile:(ki + 1) * tile, :]
            v = v_ref[0, 0, ki * tile:(ki + 1) * tile, :]
            s = _dot_nt(q, k)
            if ki == qi:
                s = jnp.where(causal, s, NEG_BIG)
            s_max = jnp.max(s, axis=-1, keepdims=True)
            if m is None:
                m = s_max
                p = jnp.exp(s - m)
                l = jnp.sum(p, axis=-1, keepdims=True)
                acc = _dot(p.astype(BF16), v)
            else:
                m_new = jnp.maximum(m, s_max)
                alpha = jnp.exp(m - m_new)
                p = jnp.exp(s - m_new)
                l = alpha * l + jnp.sum(p, axis=-1, keepdims=True)
                acc = alpha * acc + _dot(p.astype(BF16), v)
                m = m_new
        o_ref[0, qi * tile:(qi + 1) * tile, :] = (acc / l).astype(o_ref.dtype)


def _attn_prompt(q, k, v):
    batch, heads, seq, dq = q.shape
    v_head = v.shape[-1]
    tile = min(ATTN_TILE, seq)
    bh_map = lambda b, h: (b, h, 0, 0)
    return pl.pallas_call(
        functools.partial(_attn_prompt_kernel, seq=seq, tile=tile),
        grid=(batch, heads),
        in_specs=[pl.BlockSpec((1, 1, seq, dq), bh_map), pl.BlockSpec((1, 1, seq, dq), bh_map),
                  pl.BlockSpec((1, 1, seq, v_head), bh_map)],
        out_specs=pl.BlockSpec((1, seq, v_head), lambda b, h: (b, 0, h)),
        out_shape=jax.ShapeDtypeStruct((batch, seq, heads * v_head), BF16),
        compiler_params=_cparams(2),
        name="attn_prompt",
    )(q, k, v)


def _attn_out_kernel(*refs, heads, absorbed):
    if absorbed:
        o_ref, x_ref, g_ref, wuv_ref, wout_ref, xo_ref = refs
        o = jnp.concatenate([_dot(o_ref[hd], wuv_ref[hd]) for hd in range(heads)], axis=1).astype(BF16)
    else:
        o_ref, x_ref, g_ref, wout_ref, xo_ref = refs
        o = o_ref[...]
    y = _dot(o, wout_ref[...])
    xo_ref[...] = x_ref[...] + _rms(y, g_ref[1:2, :])


def _attn_out_prompt(o2d, x2d, g, wout):
    n_rows, d_model = x2d.shape
    rows = min(ROW_TILE, n_rows)
    row_map = lambda i: (i, 0)
    return pl.pallas_call(
        functools.partial(_attn_out_kernel, heads=0, absorbed=False),
        grid=(n_rows // rows,),
        in_specs=[pl.BlockSpec((rows, o2d.shape[1]), row_map), pl.BlockSpec((rows, d_model), row_map),
                  _const_spec(g.shape), _const_spec(wout.shape)],
        out_specs=pl.BlockSpec((rows, d_model), row_map),
        out_shape=jax.ShapeDtypeStruct(x2d.shape, F32),
        compiler_params=_cparams(1),
        name="attn_out_prompt",
    )(o2d, x2d, g, wout)


def _attn_out_sample(o_lat, x2d, g, wuv, wout):
    heads = o_lat.shape[0]
    args = (o_lat, x2d, g, wuv, wout)
    return pl.pallas_call(
        functools.partial(_attn_out_kernel, heads=heads, absorbed=True),
        grid=(1,),
        in_specs=[_const_spec(a.shape) for a in args],
        out_specs=_full_spec(x2d.shape),
        out_shape=jax.ShapeDtypeStruct(x2d.shape, F32),
        compiler_params=_cparams(1),
        name="attn_out_sample",
    )(*args)


def _attn_decode_kernel(*refs, pages, n_new, n_chunks):
    pt_ref = refs[0]
    del pt_ref
    qlat_ref, qrot_ref, cnew_ref, rnew_ref = refs[1:5]
    ck_refs = refs[5:5 + pages]
    kr_refs = refs[5 + pages:5 + 2 * pages]
    o_ref, m_ref, l_ref, acc_ref = refs[5 + 2 * pages:]
    c = pl.program_id(1)

    @pl.when(c == 0)
    def _():
        m_ref[...] = jnp.full(m_ref.shape, NEG_BIG, F32)
        l_ref[...] = jnp.zeros(l_ref.shape, F32)
        acc_ref[...] = jnp.zeros(acc_ref.shape, F32)

    q = qlat_ref[0]
    qr = qrot_ref[0]
    kc = jnp.concatenate([r[...].astype(BF16) for r in ck_refs], axis=0)
    kr = jnp.concatenate([r[...].astype(BF16) for r in kr_refs], axis=0)
    s = _dot_nt(q, kc) + _dot_nt(qr, kr)
    m_old = m_ref[...]
    m_new = jnp.maximum(m_old, jnp.max(s, axis=-1, keepdims=True))
    alpha = jnp.exp(m_old - m_new)
    p = jnp.exp(s - m_new)
    l_ref[...] = alpha * l_ref[...] + jnp.sum(p, axis=-1, keepdims=True)
    acc_ref[...] = alpha * acc_ref[...] + _dot(p.astype(BF16), kc)
    m_ref[...] = m_new

    @pl.when(c == n_chunks - 1)
    def _():
        qf = q.astype(F32)
        qrf = qr.astype(F32)
        n_rows = qf.shape[0]
        t_of_row = lax.rem(lax.broadcasted_iota(jnp.int32, (n_rows, 1), 0), n_new)
        s_new = []
        for t in range(n_new):
            st = (jnp.sum(qf * cnew_ref[0, t:t + 1, :], axis=-1, keepdims=True)
                  + jnp.sum(qrf * rnew_ref[0, t:t + 1, :], axis=-1, keepdims=True))
            s_new.append(jnp.where(t_of_row >= t, st, NEG_BIG))
        m_prev = m_ref[...]
        m_fin = m_prev
        for st in s_new:
            m_fin = jnp.maximum(m_fin, st)
        a = jnp.exp(m_prev - m_fin)
        l_fin = a * l_ref[...]
        acc = a * acc_ref[...]
        for t, st in enumerate(s_new):
            pt = jnp.where(t_of_row >= t, jnp.exp(st - m_fin), 0.0)
            l_fin = l_fin + pt
            acc = acc + pt * cnew_ref[0, t:t + 1, :]
        o_ref[0] = (acc / l_fin).astype(o_ref.dtype)


def _attn_decode(page_table_flat, qlat, qrot, cnew, rnew, cache_ckv, cache_kr, layer, n_pages):
    dec_batch, n_rows, kv_lora = qlat.shape
    rope = qrot.shape[-1]
    n_new = cnew.shape[1]
    page = cache_ckv.shape[2]
    pages = max(p for p in range(1, MAX_PAGES_PER_STEP + 1) if n_pages % p == 0)
    n_chunks = n_pages // pages

    def page_spec(width, p):
        return pl.BlockSpec((None, None, page, width),
                            lambda b, c, pt: (layer, pt[b * n_pages + c * pages + p], 0, 0))

    per_b = lambda b, c, pt: (b, 0, 0)
    grid_spec = pltpu.PrefetchScalarGridSpec(
        num_scalar_prefetch=1,
        grid=(dec_batch, n_chunks),
        in_specs=[pl.BlockSpec((1, n_rows, kv_lora), per_b), pl.BlockSpec((1, n_rows, rope), per_b),
                  pl.BlockSpec((1, n_new, kv_lora), per_b), pl.BlockSpec((1, n_new, rope), per_b)]
                 + [page_spec(kv_lora, p) for p in range(pages)]
                 + [page_spec(rope, p) for p in range(pages)],
        out_specs=pl.BlockSpec((1, n_rows, kv_lora), per_b),
        scratch_shapes=[pltpu.VMEM((n_rows, 1), F32), pltpu.VMEM((n_rows, 1), F32),
                        pltpu.VMEM((n_rows, kv_lora), F32)],
    )
    return pl.pallas_call(
        functools.partial(_attn_decode_kernel, pages=pages, n_new=n_new, n_chunks=n_chunks),
        grid_spec=grid_spec,
        out_shape=jax.ShapeDtypeStruct((dec_batch, n_rows, kv_lora), BF16),
        compiler_params=_cparams(2),
        name="attn_decode",
    )(page_table_flat, qlat, qrot, cnew, rnew, *([cache_ckv] * pages), *([cache_kr] * pages))


def _rope_tables(pos, rope):
    half = rope // 2
    freqs = ROPE_THETA ** (-jnp.arange(half, dtype=F32) / half)
    ang = pos.astype(F32)[:, None] * freqs[None, :]
    cos, sin = jnp.cos(ang), jnp.sin(ang)
    reps = LANES // rope
    return (jnp.concatenate([cos, cos] * reps, axis=1), jnp.concatenate([-sin, sin] * reps, axis=1))


def _swap_halves(w):
    half = w.shape[-1] // 2
    return jnp.concatenate([w[..., half:], w[..., :half]], axis=-1)


def kernel(x_prompt, x_sample, state_pool, state_sconv, cache_ckv, cache_krope, state_ffn, page_table,
           norms, w_in_even, w_pool_map, pool_scale, w_sconv, w_out_even, w_in_odd, q_norm, kv_norm,
           w_uq, w_uk, w_uv, w_out_odd, w_ffn_up, w_ffn_conv, w_ffn_down):
    batch, seq, d_model = x_prompt.shape
    dec_batch, n_new, _ = x_sample.shape
    depth = norms.shape[0]
    n_pages = page_table.shape[1]
    page = cache_ckv.shape[2]
    past_len = n_pages * page
    q_lora, heads, qk_head = w_uq.shape[1:]
    kv_lora, _, qk_nope = w_uk.shape[1:]
    v_head = w_uv.shape[-1]
    rope = qk_head - qk_nope
    d_pool = state_pool.shape[-1]
    d_ff = w_ffn_down.shape[1]
    n_groups = len(POOL_WINDOWS)
    pool_ch = d_pool // n_groups
    attn_scale = float(qk_head) ** -0.5
    assert rope * 2 == LANES and qk_nope == LANES and d_pool % (n_groups * LANES) == 0
    assert d_ff % (FFN_CHUNKS * LANES) == 0 and dec_batch % SUBLANES == 0 and seq % SUBLANES == 0

    xp = x_prompt.reshape(batch * seq, d_model)
    xs = x_sample.transpose(1, 0, 2).reshape(n_new * dec_batch, d_model)
    page_flat = page_table.reshape(-1)
    cos_p, sin_p = _rope_tables(jnp.arange(seq), rope)
    cos_s, sin_s = _rope_tables(jnp.repeat(past_len + jnp.arange(n_new), dec_batch), rope)

    outs = {k: [] for k in ("pool_p", "pool_s", "conv_p", "conv_s", "ckv_p", "kr_p", "ckv_s", "kr_s",
                            "ffn_p", "ffn_s")}
    for i in range(depth):
        g_mix, g_ffn = norms[i, 0:2], norms[i, 2:4]
        if i % 2 == 0:
            e = i // 2
            win = w_in_even[e].astype(BF16)
            wmap = jnp.zeros((d_pool, d_pool), F32)
            for gi in range(n_groups):
                wmap = wmap.at[gi * pool_ch:(gi + 1) * pool_ch, gi * pool_ch:(gi + 1) * pool_ch].set(
                    w_pool_map[e, gi])
            wmap = wmap.astype(BF16)
            scale = pool_scale[e].reshape(1, d_pool)
            wout = w_out_even[e].astype(BF16)
            xp, pool_p, conv_p = _even_prompt(xp, batch, seq, g_mix, win, wmap, scale, w_sconv[e], wout)
            pool_prev = state_pool[e].transpose(1, 0, 2).reshape(POOL_BUF * dec_batch, d_pool)
            conv_prev = state_sconv[e].transpose(1, 0, 2).reshape(2 * dec_batch, d_pool)
            xs, pool_s, conv_s = _even_sample(xs, dec_batch, g_mix, win, wmap, scale, w_sconv[e], wout,
                                              pool_prev, conv_prev)
            outs["pool_p"].append(pool_p)
            outs["conv_p"].append(conv_p)
            outs["pool_s"].append(pool_s.reshape(POOL_BUF, dec_batch, d_pool).transpose(1, 0, 2))
            outs["conv_s"].append(conv_s.reshape(2, dec_batch, d_pool).transpose(1, 0, 2))
        else:
            o = i // 2
            w_kr = w_in_odd[o][:, q_lora + kv_lora:]
            pad = jnp.zeros((d_model, LANES - rope), F32)
            win = jnp.concatenate([w_in_odd[o][:, :q_lora + kv_lora], w_kr, pad, _swap_halves(w_kr), pad],
                                  axis=1).astype(BF16)
            wq_nope = w_uq[o][:, :, :qk_nope].reshape(q_lora, heads * qk_nope)
            wq_rope = w_uq[o][:, :, qk_nope:]
            wq = jnp.concatenate([wq_nope, wq_rope.reshape(q_lora, heads * rope),
                                  _swap_halves(wq_rope).reshape(q_lora, heads * rope)], axis=1).astype(BF16)
            qn = q_norm[o].reshape(1, q_lora)
            kvn = kv_norm[o].reshape(1, kv_lora)
            wout = w_out_odd[o].astype(BF16)
            wk = w_uk[o].reshape(kv_lora, heads * qk_nope).astype(BF16)
            wv = w_uv[o].reshape(kv_lora, heads * v_head).astype(BF16)
            ckv_p, kr_p, q, k, v = _mla_proj_prompt(xp, batch, seq, g_mix, win, qn, kvn, wq, wk, wv,
                                                    cos_p, sin_p, heads=heads, rope=rope,
                                                    attn_scale=attn_scale)
            o_p = _attn_prompt(q, k, v)
            xp = _attn_out_prompt(o_p.reshape(batch * seq, heads * v_head), xp, g_mix, wout)
            outs["ckv_p"].append(ckv_p.reshape(batch, seq, kv_lora))
            outs["kr_p"].append(kr_p.reshape(batch, seq, rope))
            wukt = w_uk[o].transpose(1, 2, 0).astype(BF16)
            wuv = w_uv[o].transpose(1, 0, 2).astype(BF16)
            ckv_s, kr_s, qlat, qrot = _mla_proj_sample(xs, g_mix, win, qn, kvn, wq, wukt, cos_s, sin_s,
                                                       heads=heads, rope=rope, attn_scale=attn_scale)
            ckv_s = ckv_s.reshape(n_new, dec_batch, kv_lora).transpose(1, 0, 2)
            kr_s = kr_s.reshape(n_new, dec_batch, rope).transpose(1, 0, 2)
            qlat = qlat.reshape(heads, n_new, dec_batch, kv_lora).transpose(2, 0, 1, 3).reshape(
                dec_batch, heads * n_new, kv_lora)
            qrot = qrot.reshape(n_new, dec_batch, heads, rope).transpose(1, 2, 0, 3).reshape(
                dec_batch, heads * n_new, rope)
            o_lat = _attn_decode(page_flat, qlat, qrot, ckv_s, kr_s, cache_ckv, cache_krope, o, n_pages)
            o_lat = o_lat.reshape(dec_batch, heads, n_new, kv_lora).transpose(1, 2, 0, 3).reshape(
                heads, n_new * dec_batch, kv_lora)
            xs = _attn_out_sample(o_lat, xs, g_mix, wuv, wout)
            outs["ckv_s"].append(ckv_s)
            outs["kr_s"].append(kr_s)
        wup = w_ffn_up[i].astype(BF16)
        wdown = w_ffn_down[i].astype(BF16)
        xp, ffn_p = _ffn_prompt(xp, batch, seq, g_ffn, wup, w_ffn_conv[i], wdown)
        ffn_prev = state_ffn[i].transpose(1, 0, 2).reshape(2 * dec_batch, d_ff)
        xs, ffn_s = _ffn_sample(xs, dec_batch, g_ffn, wup, w_ffn_conv[i], wdown, ffn_prev)
        outs["ffn_p"].append(ffn_p)
        outs["ffn_s"].append(ffn_s.reshape(2, dec_batch, d_ff).transpose(1, 0, 2))

    y_prompt = xp.reshape(batch, seq, d_model)
    y_sample = xs.reshape(n_new, dec_batch, d_model).transpose(1, 0, 2)
    st = {k: jnp.stack(v) for k, v in outs.items()}
    return (y_prompt, y_sample, st["pool_p"], st["pool_s"], st["conv_p"], st["conv_s"],
            st["ckv_p"], st["kr_p"], st["ckv_s"], st["kr_s"], st["ffn_p"], st["ffn_s"])
```

```python
import functools

import jax
import jax.numpy as jnp
from jax import lax
from jax.experimental import pallas as pl
from jax.experimental.pallas import tpu as pltpu

F32 = jnp.float32
BF16 = jnp.bfloat16

NORM_EPS = 1e-6
POOL_WINDOWS = (2, 4, 8, 16)
POOL_BUF = max(POOL_WINDOWS) - 1
ROPE_THETA = 10000.0
LANES = 128
SUBLANES = 8
BF16_SUBLANES = 16
MXU_TILE = 256
LOG2E = 1.4426950408889634
NEG_BIG = -1e30

ROW_TILE = 512
PROJ_ROW_TILE = 256
ATTN_TILE = 512
FFN_CHUNKS = 2
MAX_PAGES_PER_STEP = 16
VMEM_LIMIT = 56 * 1024 * 1024


def _cparams(n_axes):
    return pltpu.CompilerParams(dimension_semantics=("arbitrary",) * n_axes, vmem_limit_bytes=VMEM_LIMIT)


def _const_spec(shape):
    n = len(shape)
    return pl.BlockSpec(shape, lambda *_: (0,) * n, pipeline_mode=pl.Buffered(1))


def _full_spec(shape):
    n = len(shape)
    return pl.BlockSpec(shape, lambda *_: (0,) * n)


def _rms(x, g):
    inv = lax.rsqrt(jnp.mean(x * x, axis=-1, keepdims=True) + NORM_EPS)
    return x * inv * g


def _dot(a, b):
    return jnp.dot(a, b, preferred_element_type=F32)


def _dot_nt(a, b):
    return lax.dot_general(a, b, (((1,), (1,)), ((), ())), preferred_element_type=F32)


def _even_kernel(*refs, rows, stride, halo_u, halo_v, prompt):
    if prompt:
        (x_ref, g_ref, win_ref, wmap_ref, scale_ref, wconv_ref, wout_ref,
         xo_ref, pool_ref, conv_ref, ubuf, vbuf) = refs
    else:
        (x_ref, g_ref, win_ref, wmap_ref, scale_ref, wconv_ref, wout_ref, pool_prev_ref, conv_prev_ref,
         xo_ref, pool_ref, conv_ref, ubuf, vbuf) = refs
    d_pool = ubuf.shape[1]
    pool_ch = d_pool // len(POOL_WINDOWS)

    if prompt:
        j = pl.program_id(1)

        @pl.when(j == 0)
        def _():
            ubuf[0:halo_u, :] = jnp.zeros((halo_u, d_pool), F32)
            vbuf[0:halo_v, :] = jnp.zeros((halo_v, d_pool), F32)
    else:
        ubuf[0:halo_u, :] = pool_prev_ref[...]
        vbuf[0:halo_v, :] = conv_prev_ref[...]

    x = x_ref[...]
    xn = _rms(x, g_ref[0:1, :])
    h = _dot(xn.astype(BF16), win_ref[...])
    u = h[:, 0:d_pool]
    gate_b = h[:, d_pool:2 * d_pool]
    gate_c = h[:, 2 * d_pool:3 * d_pool]
    hx = h[:, 3 * d_pool:4 * d_pool]
    ubuf[halo_u:halo_u + rows, :] = u
    vbuf[halo_v:halo_v + rows, :] = gate_c * hx

    if prompt:
        pos = j * rows + lax.broadcasted_iota(jnp.int32, (rows, 1), 0)
    pieces = []
    for g, w in enumerate(POOL_WINDOWS):
        cols = slice(g * pool_ch, (g + 1) * pool_ch)
        cur = ubuf[halo_u:halo_u + rows, cols]
        total = cur
        for s in range(1, w):
            total = total + ubuf[halo_u - s * stride:halo_u - s * stride + rows, cols]
        if prompt:
            count = jnp.minimum(pos + 1, w).astype(F32)
            pieces.append(total / count - cur)
        else:
            pieces.append(total / float(w) - cur)
    d = jnp.concatenate(pieces, axis=1)
    y_pool = _dot(d.astype(BF16), wmap_ref[...]) * scale_ref[...]

    conv = (wconv_ref[0:1, :] * vbuf[halo_v - 2 * stride:halo_v - 2 * stride + rows, :]
            + wconv_ref[1:2, :] * vbuf[halo_v - stride:halo_v - stride + rows, :]
            + wconv_ref[2:3, :] * vbuf[halo_v:halo_v + rows, :])
    y_conv = gate_b * conv
    y = _dot(jnp.concatenate([y_pool, y_conv], axis=1).astype(BF16), wout_ref[...])
    xo_ref[...] = x + _rms(y, g_ref[1:2, :])

    new_pool = ubuf[halo_u + rows - POOL_BUF * stride:halo_u + rows, :]
    new_conv = vbuf[halo_v + rows - 2 * stride:halo_v + rows, :]
    if prompt:
        pool_ref[0] = new_pool
        conv_ref[0] = new_conv
        ubuf[0:halo_u, :] = ubuf[rows:rows + halo_u, :]
        vbuf[0:halo_v, :] = vbuf[rows:rows + halo_v, :]
    else:
        pool_ref[...] = new_pool
        conv_ref[...] = new_conv


def _even_prompt(x2d, batch, seq, g, win, wmap, scale, wconv, wout):
    d_model = x2d.shape[1]
    d_pool = wmap.shape[0]
    rows = min(ROW_TILE, seq)
    n_t = seq // rows
    halo_u, halo_v = 2 * SUBLANES, SUBLANES
    row_spec = pl.BlockSpec((rows, d_model), lambda b, j: (b * n_t + j, 0))
    return pl.pallas_call(
        functools.partial(_even_kernel, rows=rows, stride=1, halo_u=halo_u, halo_v=halo_v, prompt=True),
        grid=(batch, n_t),
        in_specs=[row_spec, _const_spec(g.shape), _const_spec(win.shape), _const_spec(wmap.shape),
                  _const_spec(scale.shape), _const_spec(wconv.shape), _const_spec(wout.shape)],
        out_specs=[row_spec,
                   pl.BlockSpec((1, POOL_BUF, d_pool), lambda b, j: (b, 0, 0)),
                   pl.BlockSpec((1, 2, d_pool), lambda b, j: (b, 0, 0))],
        out_shape=[jax.ShapeDtypeStruct(x2d.shape, F32),
                   jax.ShapeDtypeStruct((batch, POOL_BUF, d_pool), F32),
                   jax.ShapeDtypeStruct((batch, 2, d_pool), F32)],
        scratch_shapes=[pltpu.VMEM((halo_u + rows, d_pool), F32), pltpu.VMEM((halo_v + rows, d_pool), F32)],
        compiler_params=_cparams(2),
        name="even_prompt",
    )(x2d, g, win, wmap, scale, wconv, wout)


def _even_sample(x2d, dec_batch, g, win, wmap, scale, wconv, wout, pool_prev, conv_prev):
    rows, d_model = x2d.shape
    d_pool = wmap.shape[0]
    halo_u, halo_v = POOL_BUF * dec_batch, 2 * dec_batch
    args = (x2d, g, win, wmap, scale, wconv, wout, pool_prev, conv_prev)
    return pl.pallas_call(
        functools.partial(_even_kernel, rows=rows, stride=dec_batch, halo_u=halo_u, halo_v=halo_v,
                          prompt=False),
        grid=(1,),
        in_specs=[_const_spec(a.shape) for a in args],
        out_specs=[_full_spec(x2d.shape), _full_spec(pool_prev.shape), _full_spec(conv_prev.shape)],
        out_shape=[jax.ShapeDtypeStruct(x2d.shape, F32),
                   jax.ShapeDtypeStruct(pool_prev.shape, F32),
                   jax.ShapeDtypeStruct(conv_prev.shape, F32)],
        scratch_shapes=[pltpu.VMEM((halo_u + rows, d_pool), F32), pltpu.VMEM((halo_v + rows, d_pool), F32)],
        compiler_params=_cparams(1),
        name="even_sample",
    )(*args)


def _ffn_kernel(*refs, rows, stride, halo, prompt):
    if prompt:
        x_ref, g_ref, wup_ref, wconv_ref, wdown_ref, xo_ref, state_ref, gbuf = refs
    else:
        x_ref, g_ref, wup_ref, wconv_ref, wdown_ref, prev_ref, xo_ref, state_ref, gbuf = refs
    d_ff = gbuf.shape[1]
    n_tiles = d_ff // MXU_TILE
    bounds = [MXU_TILE * ((n_tiles * c + FFN_CHUNKS - 1) // FFN_CHUNKS) for c in range(FFN_CHUNKS + 1)]

    if prompt:
        @pl.when(pl.program_id(1) == 0)
        def _():
            gbuf[0:halo, :] = jnp.zeros((halo, d_ff), F32)
    else:
        gbuf[0:halo, :] = prev_ref[...]

    x = x_ref[...]
    xn = _rms(x, g_ref[0:1, :]).astype(BF16)
    y = None
    for c in range(FFN_CHUNKS):
        lo, hi = bounds[c], bounds[c + 1]
        cols = slice(lo, hi)
        gate = _dot(xn, wup_ref[:, lo:hi])
        up = _dot(xn, wup_ref[:, d_ff + lo:d_ff + hi])
        gbuf[halo:halo + rows, cols] = gate
        conv = (wconv_ref[0:1, cols] * gbuf[halo - 2 * stride:halo - 2 * stride + rows, cols]
                + wconv_ref[1:2, cols] * gbuf[halo - stride:halo - stride + rows, cols]
                + wconv_ref[2:3, cols] * gate)
        act = conv * jax.nn.sigmoid(conv) * up
        part = _dot(act.astype(BF16), wdown_ref[lo:hi, :])
        y = part if y is None else y + part
    xo_ref[...] = x + _rms(y, g_ref[1:2, :])

    new_state = gbuf[halo + rows - 2 * stride:halo + rows, :]
    if prompt:
        state_ref[0] = new_state
        gbuf[0:halo, :] = gbuf[rows:rows + halo, :]
    else:
        state_ref[...] = new_state


def _ffn_prompt(x2d, batch, seq, g, wup, wconv, wdown):
    d_model = x2d.shape[1]
    d_ff = wdown.shape[0]
    rows = min(ROW_TILE, seq)
    n_t = seq // rows
    halo = SUBLANES
    row_spec = pl.BlockSpec((rows, d_model), lambda b, j: (b * n_t + j, 0))
    return pl.pallas_call(
        functools.partial(_ffn_kernel, rows=rows, stride=1, halo=halo, prompt=True),
        grid=(batch, n_t),
        in_specs=[row_spec, _const_spec(g.shape), _const_spec(wup.shape), _const_spec(wconv.shape),
                  _const_spec(wdown.shape)],
        out_specs=[row_spec, pl.BlockSpec((1, 2, d_ff), lambda b, j: (b, 0, 0))],
        out_shape=[jax.ShapeDtypeStruct(x2d.shape, F32), jax.ShapeDtypeStruct((batch, 2, d_ff), F32)],
        scratch_shapes=[pltpu.VMEM((halo + rows, d_ff), F32)],
        compiler_params=_cparams(2),
        name="ffn_prompt",
    )(x2d, g, wup, wconv, wdown)


def _ffn_sample(x2d, dec_batch, g, wup, wconv, wdown, prev):
    rows = x2d.shape[0]
    d_ff = wdown.shape[0]
    halo = 2 * dec_batch
    args = (x2d, g, wup, wconv, wdown, prev)
    return pl.pallas_call(
        functools.partial(_ffn_kernel, rows=rows, stride=dec_batch, halo=halo, prompt=False),
        grid=(1,),
        in_specs=[_const_spec(a.shape) for a in args],
        out_specs=[_full_spec(x2d.shape), _full_spec(prev.shape)],
        out_shape=[jax.ShapeDtypeStruct(x2d.shape, F32), jax.ShapeDtypeStruct(prev.shape, F32)],
        scratch_shapes=[pltpu.VMEM((halo + rows, d_ff), F32)],
        compiler_params=_cparams(1),
        name="ffn_sample",
    )(*args)


def _mla_proj_kernel(*refs, heads, q_lora, kv_lora, qk_nope, absorbed, attn_scale):
    if absorbed:
        (x_ref, g_ref, win_ref, qn_ref, kvn_ref, wq_ref, wukt_ref, cos_ref, sin_ref,
         ckv_ref, kr_ref, qlat_ref, qrot_ref) = refs
    else:
        (x_ref, g_ref, win_ref, qn_ref, kvn_ref, wq_ref, wk_ref, wvt_ref, cos_ref, sin_ref,
         ckv_ref, kr_ref, q_ref, k_ref, vt_ref) = refs
    rope = kr_ref.shape[-1]
    cos = cos_ref[...]
    sin = sin_ref[...]

    xn = _rms(x_ref[...], g_ref[0:1, :])
    h = _dot(xn.astype(BF16), win_ref[...])
    c_q = _rms(h[:, 0:q_lora], qn_ref[...]) * attn_scale
    c_kv = _rms(h[:, q_lora:q_lora + kv_lora], kvn_ref[...])
    base = q_lora + kv_lora
    kr_lo = h[:, base:base + LANES] * cos + h[:, base + LANES:base + 2 * LANES] * sin
    ckv_ref[...] = c_kv
    kr_ref[...] = kr_lo[:, 0:rope]

    q_all = _dot(c_q.astype(BF16), wq_ref[...])
    n_nope = heads * qk_nope
    n_rope = heads * rope
    q_rot = [q_all[:, n_nope + p * LANES:n_nope + (p + 1) * LANES] * cos
             + q_all[:, n_nope + n_rope + p * LANES:n_nope + n_rope + (p + 1) * LANES] * sin
             for p in range(n_rope // LANES)]

    if absorbed:
        for hd in range(heads):
            q_nope = q_all[:, hd * qk_nope:(hd + 1) * qk_nope].astype(BF16)
            qlat_ref[hd] = _dot(q_nope, wukt_ref[hd]).astype(BF16)
        qrot_ref[...] = jnp.concatenate(q_rot, axis=1).astype(BF16)
    else:
        kr_hi = pltpu.roll(kr_lo, LANES // 2, axis=1)
        ckv_b = c_kv.astype(BF16)
        k_nope = _dot(ckv_b, wk_ref[...])
        vt_all = _dot_nt(wvt_ref[...], ckv_b)
        v_head = vt_ref.shape[2]
        per_pair = LANES // rope
        for hd in range(heads):
            q_ref[0, hd] = jnp.concatenate(
                [q_all[:, hd * qk_nope:(hd + 1) * qk_nope], q_rot[hd // per_pair]], axis=1).astype(BF16)
            k_ref[0, hd] = jnp.concatenate(
                [k_nope[:, hd * qk_nope:(hd + 1) * qk_nope], kr_lo if hd % per_pair == 0 else kr_hi],
                axis=1).astype(BF16)
            vt_ref[0, hd] = vt_all[hd * v_head:(hd + 1) * v_head, :].astype(BF16)


def _mla_proj_prompt(x2d, batch, seq, g, win, qn, kvn, wq, wk, wvt, cos, sin, *, heads, rope, attn_scale):
    d_model = x2d.shape[1]
    q_lora, kv_lora = qn.shape[1], kvn.shape[1]
    qk_nope = wk.shape[1] // heads
    v_head = wvt.shape[0] // heads
    rows = min(PROJ_ROW_TILE, seq)
    n_t = seq // rows
    row_map = lambda b, j: (b * n_t + j, 0)
    head_map = lambda b, j: (b, 0, j, 0)
    return pl.pallas_call(
        functools.partial(_mla_proj_kernel, heads=heads, q_lora=q_lora, kv_lora=kv_lora, qk_nope=qk_nope,
                          absorbed=False, attn_scale=attn_scale),
        grid=(batch, n_t),
        in_specs=[pl.BlockSpec((rows, d_model), row_map), _const_spec(g.shape), _const_spec(win.shape),
                  _const_spec(qn.shape), _const_spec(kvn.shape), _const_spec(wq.shape),
                  _const_spec(wk.shape), _const_spec(wvt.shape),
                  pl.BlockSpec((rows, LANES), lambda b, j: (j, 0)),
                  pl.BlockSpec((rows, LANES), lambda b, j: (j, 0))],
        out_specs=[pl.BlockSpec((rows, kv_lora), row_map), pl.BlockSpec((rows, rope), row_map),
                   pl.BlockSpec((1, heads, rows, qk_nope + LANES), head_map),
                   pl.BlockSpec((1, heads, rows, qk_nope + LANES), head_map),
                   pl.BlockSpec((1, heads, v_head, rows), lambda b, j: (b, 0, 0, j))],
        out_shape=[jax.ShapeDtypeStruct((batch * seq, kv_lora), F32),
                   jax.ShapeDtypeStruct((batch * seq, rope), F32),
                   jax.ShapeDtypeStruct((batch, heads, seq, qk_nope + LANES), BF16),
                   jax.ShapeDtypeStruct((batch, heads, seq, qk_nope + LANES), BF16),
                   jax.ShapeDtypeStruct((batch, heads, v_head, seq), BF16)],
        compiler_params=_cparams(2),
        name="mla_proj_prompt",
    )(x2d, g, win, qn, kvn, wq, wk, wvt, cos, sin)


def _mla_proj_sample(x2d, g, win, qn, kvn, wq, wukt, cos, sin, *, heads, rope, attn_scale):
    rows = x2d.shape[0]
    q_lora, kv_lora = qn.shape[1], kvn.shape[1]
    qk_nope = wukt.shape[1]
    args = (x2d, g, win, qn, kvn, wq, wukt, cos, sin)
    out_shapes = [(rows, kv_lora), (rows, rope), (heads, rows, kv_lora), (rows, heads * rope)]
    out_dtypes = [F32, F32, BF16, BF16]
    return pl.pallas_call(
        functools.partial(_mla_proj_kernel, heads=heads, q_lora=q_lora, kv_lora=kv_lora, qk_nope=qk_nope,
                          absorbed=True, attn_scale=attn_scale),
        grid=(1,),
        in_specs=[_const_spec(a.shape) for a in args],
        out_specs=[_full_spec(s) for s in out_shapes],
        out_shape=[jax.ShapeDtypeStruct(s, d) for s, d in zip(out_shapes, out_dtypes)],
        compiler_params=_cparams(1),
        name="mla_proj_sample",
    )(*args)


def _attn_prompt_kernel(q_ref, k_ref, vt_ref, o_ref, *, seq, tile):
    n_t = seq // tile
    key_i = lax.broadcasted_iota(jnp.int32, (tile, tile), 0)
    qry_i = lax.broadcasted_iota(jnp.int32, (tile, tile), 1)
    causal = key_i <= qry_i
    v_head = vt_ref.shape[2]
    ones = jnp.ones((BF16_SUBLANES, tile), BF16)
    pairs = [(qi, ki) for qi in range(n_t) for ki in range(qi + 1)]

    def scores(pair):
        qi, ki = pair
        return _dot_nt(k_ref[0, 0, ki * tile:(ki + 1) * tile, :],
                       q_ref[0, 0, qi * tile:(qi + 1) * tile, :])

    s_next = scores(pairs[0])
    m = acc = None
    for n, (qi, ki) in enumerate(pairs):
        s = s_next
        if n + 1 < len(pairs):
            s_next = scores(pairs[n + 1])
        if ki == qi:
            s = jnp.where(causal, s, NEG_BIG)
        vt = jnp.concatenate([vt_ref[0, 0, :, ki * tile:(ki + 1) * tile], ones], axis=0)
        s_max = jnp.max(s, axis=0, keepdims=True)
        m_new = s_max if ki == 0 else jnp.maximum(m, s_max)
        pv = _dot(vt, jnp.exp2(s - m_new).astype(BF16))
        acc = pv if ki == 0 else jnp.exp2(m - m_new) * acc + pv
        m = m_new
        if ki == qi:
            o_t = acc[0:v_head, :] / acc[v_head:v_head + 1, :]
            o_ref[0, qi * tile:(qi + 1) * tile, :] = o_t.T.astype(o_ref.dtype)


def _attn_prompt(q, k, vt):
    batch, heads, seq, dq = q.shape
    v_head = vt.shape[2]
    tile = min(ATTN_TILE, seq)
    bh_map = lambda b, h: (b, h, 0, 0)
    return pl.pallas_call(
        functools.partial(_attn_prompt_kernel, seq=seq, tile=tile),
        grid=(batch, heads),
        in_specs=[pl.BlockSpec((1, 1, seq, dq), bh_map), pl.BlockSpec((1, 1, seq, dq), bh_map),
                  pl.BlockSpec((1, 1, v_head, seq), bh_map)],
        out_specs=pl.BlockSpec((1, seq, v_head), lambda b, h: (b, 0, h)),
        out_shape=jax.ShapeDtypeStruct((batch, seq, heads * v_head), BF16),
        compiler_params=_cparams(2),
        name="attn_prompt",
    )(q, k, vt)


def _attn_out_kernel(*refs, heads, absorbed):
    if absorbed:
        o_ref, x_ref, g_ref, wuv_ref, wout_ref, xo_ref = refs
        o = jnp.concatenate([_dot(o_ref[hd], wuv_ref[hd]) for hd in range(heads)], axis=1).astype(BF16)
    else:
        o_ref, x_ref, g_ref, wout_ref, xo_ref = refs
        o = o_ref[...]
    y = _dot(o, wout_ref[...])
    xo_ref[...] = x_ref[...] + _rms(y, g_ref[1:2, :])


def _attn_out_prompt(o2d, x2d, g, wout):
    n_rows, d_model = x2d.shape
    rows = min(ROW_TILE, n_rows)
    row_map = lambda i: (i, 0)
    return pl.pallas_call(
        functools.partial(_attn_out_kernel, heads=0, absorbed=False),
        grid=(n_rows // rows,),
        in_specs=[pl.BlockSpec((rows, o2d.shape[1]), row_map), pl.BlockSpec((rows, d_model), row_map),
                  _const_spec(g.shape), _const_spec(wout.shape)],
        out_specs=pl.BlockSpec((rows, d_model), row_map),
        out_shape=jax.ShapeDtypeStruct(x2d.shape, F32),
        compiler_params=_cparams(1),
        name="attn_out_prompt",
    )(o2d, x2d, g, wout)


def _attn_out_sample(o_lat, x2d, g, wuv, wout):
    heads = o_lat.shape[0]
    args = (o_lat, x2d, g, wuv, wout)
    return pl.pallas_call(
        functools.partial(_attn_out_kernel, heads=heads, absorbed=True),
        grid=(1,),
        in_specs=[_const_spec(a.shape) for a in args],
        out_specs=_full_spec(x2d.shape),
        out_shape=jax.ShapeDtypeStruct(x2d.shape, F32),
        compiler_params=_cparams(1),
        name="attn_out_sample",
    )(*args)


def _attn_decode_kernel(pt_ref, qlat_ref, qrot_ref, cnew_ref, rnew_ref, ck_hbm, krt_hbm, o_ref,
                        kbuf, rbuf, kb16, sems, *, layer, n_pages, page, n_new, chunk_pages):
    b = pl.program_id(0)
    n_b = pl.num_programs(0)
    slot = lax.rem(b, 2)

    def page_copies(req, sl):
        cps = []
        for j in range(n_pages):
            pg = pt_ref[req * n_pages + j]
            cps.append(pltpu.make_async_copy(ck_hbm.at[layer, pg], kbuf.at[sl, pl.ds(j * page, page), :],
                                             sems.at[0, sl]))
            cps.append(pltpu.make_async_copy(krt_hbm.at[layer, pg], rbuf.at[sl, j], sems.at[1, sl]))
        return cps

    @pl.when(b == 0)
    def _():
        for cp in page_copies(0, 0):
            cp.start()

    @pl.when(b + 1 < n_b)
    def _():
        for cp in page_copies(b + 1, 1 - slot):
            cp.start()

    for cp in page_copies(b, slot):
        cp.wait()

    q = qlat_ref[0]
    qr = qrot_ref[0]
    chunk = chunk_pages * page
    n_chunks = n_pages // chunk_pages

    def scores(c):
        kb16[c * chunk:(c + 1) * chunk, :] = kbuf[slot, c * chunk:(c + 1) * chunk, :].astype(BF16)
        krt = jnp.concatenate([rbuf[slot, c * chunk_pages + j] for j in range(chunk_pages)],
                              axis=1).astype(BF16)
        return _dot_nt(q, kb16[c * chunk:(c + 1) * chunk, :]) + _dot(qr, krt)

    qf = q.astype(F32)
    qrf = qr.astype(F32)
    n_rows = qf.shape[0]
    t_of_row = lax.rem(lax.broadcasted_iota(jnp.int32, (n_rows, 1), 0), n_new)
    s_new = []
    for t in range(n_new):
        st = (jnp.sum(qf * cnew_ref[0, t:t + 1, :], axis=-1, keepdims=True)
              + jnp.sum(qrf * rnew_ref[0, t:t + 1, :], axis=-1, keepdims=True))
        s_new.append(jnp.where(t_of_row >= t, st, NEG_BIG))

    m = s_new[0]
    for st in s_new[1:]:
        m = jnp.maximum(m, st)
    l = jnp.zeros_like(m)
    acc = jnp.zeros((n_rows, kb16.shape[1]), F32)
    for t, st in enumerate(s_new):
        pt = jnp.where(t_of_row >= t, jnp.exp2(st - m), 0.0)
        l = l + pt
        acc = acc + pt * cnew_ref[0, t:t + 1, :]

    s_next = scores(0)
    for c in range(n_chunks):
        s = s_next
        if c + 1 < n_chunks:
            s_next = scores(c + 1)
        m_new = jnp.maximum(m, jnp.max(s, axis=-1, keepdims=True))
        alpha = jnp.exp2(m - m_new)
        p = jnp.exp2(s - m_new)
        l = alpha * l + jnp.sum(p, axis=-1, keepdims=True)
        acc = alpha * acc + _dot(p.astype(BF16), kb16[c * chunk:(c + 1) * chunk, :])
        m = m_new
    o_ref[0] = (acc / l).astype(o_ref.dtype)


def _attn_decode(page_table_flat, qlat, qrot, cnew, rnew, cache_ckv, cache_krt, layer, n_pages):
    dec_batch, n_rows, kv_lora = qlat.shape
    rope = qrot.shape[-1]
    n_new = cnew.shape[1]
    page = cache_ckv.shape[2]
    chunk_pages = max(p for p in range(1, MAX_PAGES_PER_STEP + 1) if n_pages % p == 0)
    per_b = lambda b, pt: (b, 0, 0)
    grid_spec = pltpu.PrefetchScalarGridSpec(
        num_scalar_prefetch=1,
        grid=(dec_batch,),
        in_specs=[pl.BlockSpec((1, n_rows, kv_lora), per_b), pl.BlockSpec((1, n_rows, rope), per_b),
                  pl.BlockSpec((1, n_new, kv_lora), per_b), pl.BlockSpec((1, n_new, rope), per_b),
                  pl.BlockSpec(memory_space=pl.ANY), pl.BlockSpec(memory_space=pl.ANY)],
        out_specs=pl.BlockSpec((1, n_rows, kv_lora), per_b),
        scratch_shapes=[pltpu.VMEM((2, n_pages * page, kv_lora), F32),
                        pltpu.VMEM((2, n_pages, rope, page), F32),
                        pltpu.VMEM((n_pages * page, kv_lora), BF16),
                        pltpu.SemaphoreType.DMA((2, 2))],
    )
    return pl.pallas_call(
        functools.partial(_attn_decode_kernel, layer=layer, n_pages=n_pages, page=page, n_new=n_new,
                          chunk_pages=chunk_pages),
        grid_spec=grid_spec,
        out_shape=jax.ShapeDtypeStruct((dec_batch, n_rows, kv_lora), BF16),
        compiler_params=_cparams(1),
        name="attn_decode",
    )(page_table_flat, qlat, qrot, cnew, rnew, cache_ckv, cache_krt)


def _rope_tables(pos, rope):
    half = rope // 2
    freqs = ROPE_THETA ** (-jnp.arange(half, dtype=F32) / half)
    ang = pos.astype(F32)[:, None] * freqs[None, :]
    cos, sin = jnp.cos(ang), jnp.sin(ang)
    reps = LANES // rope
    return (jnp.concatenate([cos, cos] * reps, axis=1), jnp.concatenate([-sin, sin] * reps, axis=1))


def _swap_halves(w):
    half = w.shape[-1] // 2
    return jnp.concatenate([w[..., half:], w[..., :half]], axis=-1)


def kernel(x_prompt, x_sample, state_pool, state_sconv, cache_ckv, cache_krope, state_ffn, page_table,
           norms, w_in_even, w_pool_map, pool_scale, w_sconv, w_out_even, w_in_odd, q_norm, kv_norm,
           w_uq, w_uk, w_uv, w_out_odd, w_ffn_up, w_ffn_conv, w_ffn_down):
    batch, seq, d_model = x_prompt.shape
    dec_batch, n_new, _ = x_sample.shape
    depth = norms.shape[0]
    n_pages = page_table.shape[1]
    page = cache_ckv.shape[2]
    past_len = n_pages * page
    q_lora, heads, qk_head = w_uq.shape[1:]
    kv_lora, _, qk_nope = w_uk.shape[1:]
    v_head = w_uv.shape[-1]
    rope = qk_head - qk_nope
    d_pool = state_pool.shape[-1]
    d_ff = w_ffn_down.shape[1]
    n_groups = len(POOL_WINDOWS)
    pool_ch = d_pool // n_groups
    attn_scale = float(qk_head) ** -0.5 * LOG2E
    assert rope * 2 == LANES and qk_nope == LANES and d_pool % (n_groups * LANES) == 0
    assert d_ff % MXU_TILE == 0 and dec_batch % SUBLANES == 0 and seq % SUBLANES == 0

    xp = x_prompt.reshape(batch * seq, d_model)
    xs = x_sample.transpose(1, 0, 2).reshape(n_new * dec_batch, d_model)
    page_flat = page_table.reshape(-1)
    cache_krt = cache_krope.transpose(0, 1, 3, 2)
    cos_p, sin_p = _rope_tables(jnp.arange(seq), rope)
    cos_s, sin_s = _rope_tables(jnp.repeat(past_len + jnp.arange(n_new), dec_batch), rope)

    outs = {k: [] for k in ("pool_p", "pool_s", "conv_p", "conv_s", "ckv_p", "kr_p", "ckv_s", "kr_s",
                            "ffn_p", "ffn_s")}
    for i in range(depth):
        g_mix, g_ffn = norms[i, 0:2], norms[i, 2:4]
        if i % 2 == 0:
            e = i // 2
            win = w_in_even[e].astype(BF16)
            wmap = jnp.zeros((d_pool, d_pool), F32)
            for gi in range(n_groups):
                wmap = wmap.at[gi * pool_ch:(gi + 1) * pool_ch, gi * pool_ch:(gi + 1) * pool_ch].set(
                    w_pool_map[e, gi])
            wmap = wmap.astype(BF16)
            scale = pool_scale[e].reshape(1, d_pool)
            wout = w_out_even[e].astype(BF16)
            xp, pool_p, conv_p = _even_prompt(xp, batch, seq, g_mix, win, wmap, scale, w_sconv[e], wout)
            pool_prev = state_pool[e].transpose(1, 0, 2).reshape(POOL_BUF * dec_batch, d_pool)
            conv_prev = state_sconv[e].transpose(1, 0, 2).reshape(2 * dec_batch, d_pool)
            xs, pool_s, conv_s = _even_sample(xs, dec_batch, g_mix, win, wmap, scale, w_sconv[e], wout,
                                              pool_prev, conv_prev)
            outs["pool_p"].append(pool_p)
            outs["conv_p"].append(conv_p)
            outs["pool_s"].append(pool_s.reshape(POOL_BUF, dec_batch, d_pool).transpose(1, 0, 2))
            outs["conv_s"].append(conv_s.reshape(2, dec_batch, d_pool).transpose(1, 0, 2))
        else:
            o = i // 2
            w_kr = w_in_odd[o][:, q_lora + kv_lora:]
            pad = jnp.zeros((d_model, LANES - rope), F32)
            win = jnp.concatenate([w_in_odd[o][:, :q_lora + kv_lora], w_kr, pad, _swap_halves(w_kr), pad],
                                  axis=1).astype(BF16)
            wq_nope = w_uq[o][:, :, :qk_nope].reshape(q_lora, heads * qk_nope)
            wq_rope = w_uq[o][:, :, qk_nope:]
            wq = jnp.concatenate([wq_nope, wq_rope.reshape(q_lora, heads * rope),
                                  _swap_halves(wq_rope).reshape(q_lora, heads * rope)], axis=1).astype(BF16)
            qn = q_norm[o].reshape(1, q_lora)
            kvn = kv_norm[o].reshape(1, kv_lora)
            wout = w_out_odd[o].astype(BF16)
            wk = w_uk[o].reshape(kv_lora, heads * qk_nope).astype(BF16)
            wvt = w_uv[o].transpose(1, 2, 0).reshape(heads * v_head, kv_lora).astype(BF16)
            ckv_p, kr_p, q, k, vt = _mla_proj_prompt(xp, batch, seq, g_mix, win, qn, kvn, wq, wk, wvt,
                                                     cos_p, sin_p, heads=heads, rope=rope,
                                                     attn_scale=attn_scale)
            o_p = _attn_prompt(q, k, vt)
            xp = _attn_out_prompt(o_p.reshape(batch * seq, heads * v_head), xp, g_mix, wout)
            outs["ckv_p"].append(ckv_p.reshape(batch, seq, kv_lora))
            outs["kr_p"].append(kr_p.reshape(batch, seq, rope))
            wukt = w_uk[o].transpose(1, 2, 0).astype(BF16)
            wuv = w_uv[o].transpose(1, 0, 2).astype(BF16)
            ckv_s, kr_s, qlat, qrot = _mla_proj_sample(xs, g_mix, win, qn, kvn, wq, wukt, cos_s, sin_s,
                                                       heads=heads, rope=rope, attn_scale=attn_scale)
            ckv_s = ckv_s.reshape(n_new, dec_batch, kv_lora).transpose(1, 0, 2)
            kr_s = kr_s.reshape(n_new, dec_batch, rope).transpose(1, 0, 2)
            qlat = qlat.reshape(heads, n_new, dec_batch, kv_lora).transpose(2, 0, 1, 3).reshape(
                dec_batch, heads * n_new, kv_lora)
            qrot = qrot.reshape(n_new, dec_batch, heads, rope).transpose(1, 2, 0, 3).reshape(
                dec_batch, heads * n_new, rope)
            o_lat = _attn_decode(page_flat, qlat, qrot, ckv_s, kr_s, cache_ckv, cache_krt, o, n_pages)
            o_lat = o_lat.reshape(dec_batch, heads, n_new, kv_lora).transpose(1, 2, 0, 3).reshape(
                heads, n_new * dec_batch, kv_lora)
            xs = _attn_out_sample(o_lat, xs, g_mix, wuv, wout)
            outs["ckv_s"].append(ckv_s)
            outs["kr_s"].append(kr_s)
        wup = w_ffn_up[i].astype(BF16)
        wdown = w_ffn_down[i].astype(BF16)
        xp, ffn_p = _ffn_prompt(xp, batch, seq, g_ffn, wup, w_ffn_conv[i], wdown)
        ffn_prev = state_ffn[i].transpose(1, 0, 2).reshape(2 * dec_batch, d_ff)
        xs, ffn_s = _ffn_sample(xs, dec_batch, g_ffn, wup, w_ffn_conv[i], wdown, ffn_prev)
        outs["ffn_p"].append(ffn_p)
        outs["ffn_s"].append(ffn_s.reshape(2, dec_batch, d_ff).transpose(1, 0, 2))

    y_prompt = xp.reshape(batch, seq, d_model)
    y_sample = xs.reshape(n_new, dec_batch, d_model).transpose(1, 0, 2)
    st = {k: jnp.stack(v) for k, v in outs.items()}
    return (y_prompt, y_sample, st["pool_p"], st["pool_s"], st["conv_p"], st["conv_s"],
            st["ckv_p"], st["kr_p"], st["ckv_s"], st["kr_s"], st["ffn_p"], st["ffn_s"])
```

```python
import functools

import jax
import jax.numpy as jnp
from jax import lax
from jax.experimental import pallas as pl
from jax.experimental.pallas import tpu as pltpu

F32 = jnp.float32
BF16 = jnp.bfloat16

NORM_EPS = 1e-6
POOL_WINDOWS = (2, 4, 8, 16)
POOL_BUF = max(POOL_WINDOWS) - 1
ROPE_THETA = 10000.0
LANES = 128
SUBLANES = 8
BF16_SUBLANES = 16
MXU_TILE = 256
LOG2E = 1.4426950408889634
NEG_BIG = -1e30

ROW_TILE = 512
OUT_ROW_TILE = 1024
PROJ_ROW_TILE = 512
ATTN_Q_BLOCK = 256
ATTN_K_TILE = 512
ATTN_HEADS_PER_STEP = 2
ATTN_LOOKAHEAD = 6
FFN_CHUNKS = 2
MAX_PAGES_PER_STEP = 16
VMEM_LIMIT = 56 * 1024 * 1024


def _cparams(n_axes):
    return pltpu.CompilerParams(dimension_semantics=("arbitrary",) * n_axes, vmem_limit_bytes=VMEM_LIMIT)


def _const_spec(shape):
    n = len(shape)
    return pl.BlockSpec(shape, lambda *_: (0,) * n, pipeline_mode=pl.Buffered(1))


def _layer_spec(stacked_shape, layer):
    n = len(stacked_shape) - 1
    return pl.BlockSpec((None,) + tuple(stacked_shape[1:]), lambda *_: (layer,) + (0,) * n,
                        pipeline_mode=pl.Buffered(1))


def _full_spec(shape):
    n = len(shape)
    return pl.BlockSpec(shape, lambda *_: (0,) * n)


def _rms(x, g):
    inv = lax.rsqrt(jnp.mean(x * x, axis=-1, keepdims=True) + NORM_EPS)
    return x * inv * g


def _dot(a, b):
    return jnp.dot(a, b, preferred_element_type=F32)


def _dot_nt(a, b):
    return lax.dot_general(a, b, (((1,), (1,)), ((), ())), preferred_element_type=F32)


def _even_kernel(*refs, rows, stride, halo_u, halo_v, prompt):
    if prompt:
        (x_ref, g_ref, win_ref, wmap_ref, scale_ref, wconv_ref, wout_ref,
         xo_ref, pool_ref, conv_ref, ubuf, vbuf) = refs
    else:
        (x_ref, g_ref, win_ref, wmap_ref, scale_ref, wconv_ref, wout_ref, pool_prev_ref, conv_prev_ref,
         xo_ref, pool_ref, conv_ref, ubuf, vbuf) = refs
    d_pool = ubuf.shape[1]
    pool_ch = d_pool // len(POOL_WINDOWS)

    if prompt:
        j = pl.program_id(1)

        @pl.when(j == 0)
        def _():
            ubuf[0:halo_u, :] = jnp.zeros((halo_u, d_pool), F32)
            vbuf[0:halo_v, :] = jnp.zeros((halo_v, d_pool), F32)
    else:
        ubuf[0:halo_u, :] = pool_prev_ref[...]
        vbuf[0:halo_v, :] = conv_prev_ref[...]

    x = x_ref[...]
    xn = _rms(x, g_ref[0:1, :])
    h = _dot(xn.astype(BF16), win_ref[...])
    u = h[:, 0:d_pool]
    gate_b = h[:, d_pool:2 * d_pool]
    gate_c = h[:, 2 * d_pool:3 * d_pool]
    hx = h[:, 3 * d_pool:4 * d_pool]
    ubuf[halo_u:halo_u + rows, :] = u
    vbuf[halo_v:halo_v + rows, :] = gate_c * hx

    if prompt:
        pos = j * rows + lax.broadcasted_iota(jnp.int32, (rows, 1), 0)
    pieces = []
    for g, w in enumerate(POOL_WINDOWS):
        cols = slice(g * pool_ch, (g + 1) * pool_ch)
        cur = ubuf[halo_u:halo_u + rows, cols]
        total = cur
        for s in range(1, w):
            total = total + ubuf[halo_u - s * stride:halo_u - s * stride + rows, cols]
        if prompt:
            count = jnp.minimum(pos + 1, w).astype(F32)
            pieces.append(total / count - cur)
        else:
            pieces.append(total / float(w) - cur)
    d = jnp.concatenate(pieces, axis=1)
    y_pool = _dot(d.astype(BF16), wmap_ref[...]) * scale_ref[...]

    conv = (wconv_ref[0:1, :] * vbuf[halo_v - 2 * stride:halo_v - 2 * stride + rows, :]
            + wconv_ref[1:2, :] * vbuf[halo_v - stride:halo_v - stride + rows, :]
            + wconv_ref[2:3, :] * vbuf[halo_v:halo_v + rows, :])
    y_conv = gate_b * conv
    y = _dot(jnp.concatenate([y_pool, y_conv], axis=1).astype(BF16), wout_ref[...])
    xo_ref[...] = x + _rms(y, g_ref[1:2, :])

    new_pool = ubuf[halo_u + rows - POOL_BUF * stride:halo_u + rows, :]
    new_conv = vbuf[halo_v + rows - 2 * stride:halo_v + rows, :]
    if prompt:
        pool_ref[0] = new_pool
        conv_ref[0] = new_conv
        ubuf[0:halo_u, :] = ubuf[rows:rows + halo_u, :]
        vbuf[0:halo_v, :] = vbuf[rows:rows + halo_v, :]
    else:
        pool_ref[...] = new_pool
        conv_ref[...] = new_conv


def _even_prompt(x2d, batch, seq, g, win, wmap, scale, wconv, wout, layer):
    d_model = x2d.shape[1]
    d_pool = wmap.shape[0]
    rows = min(ROW_TILE, seq)
    n_t = seq // rows
    halo_u, halo_v = 2 * SUBLANES, SUBLANES
    row_spec = pl.BlockSpec((rows, d_model), lambda b, j: (b * n_t + j, 0))
    return pl.pallas_call(
        functools.partial(_even_kernel, rows=rows, stride=1, halo_u=halo_u, halo_v=halo_v, prompt=True),
        grid=(batch, n_t),
        in_specs=[row_spec, _const_spec(g.shape), _layer_spec(win.shape, layer), _const_spec(wmap.shape),
                  _const_spec(scale.shape), _const_spec(wconv.shape), _layer_spec(wout.shape, layer)],
        out_specs=[row_spec,
                   pl.BlockSpec((1, POOL_BUF, d_pool), lambda b, j: (b, 0, 0)),
                   pl.BlockSpec((1, 2, d_pool), lambda b, j: (b, 0, 0))],
        out_shape=[jax.ShapeDtypeStruct(x2d.shape, F32),
                   jax.ShapeDtypeStruct((batch, POOL_BUF, d_pool), F32),
                   jax.ShapeDtypeStruct((batch, 2, d_pool), F32)],
        scratch_shapes=[pltpu.VMEM((halo_u + rows, d_pool), F32), pltpu.VMEM((halo_v + rows, d_pool), F32)],
        compiler_params=_cparams(2),
        name="even_prompt",
    )(x2d, g, win, wmap, scale, wconv, wout)


def _even_sample(x2d, dec_batch, g, win, wmap, scale, wconv, wout, pool_prev, conv_prev, layer):
    rows, d_model = x2d.shape
    d_pool = wmap.shape[0]
    halo_u, halo_v = POOL_BUF * dec_batch, 2 * dec_batch
    args = (x2d, g, win, wmap, scale, wconv, wout, pool_prev, conv_prev)
    return pl.pallas_call(
        functools.partial(_even_kernel, rows=rows, stride=dec_batch, halo_u=halo_u, halo_v=halo_v,
                          prompt=False),
        grid=(1,),
        in_specs=[_layer_spec(a.shape, layer) if a is win or a is wout else _const_spec(a.shape)
                  for a in args],
        out_specs=[_full_spec(x2d.shape), _full_spec(pool_prev.shape), _full_spec(conv_prev.shape)],
        out_shape=[jax.ShapeDtypeStruct(x2d.shape, F32),
                   jax.ShapeDtypeStruct(pool_prev.shape, F32),
                   jax.ShapeDtypeStruct(conv_prev.shape, F32)],
        scratch_shapes=[pltpu.VMEM((halo_u + rows, d_pool), F32), pltpu.VMEM((halo_v + rows, d_pool), F32)],
        compiler_params=_cparams(1),
        name="even_sample",
    )(*args)


def _ffn_kernel(*refs, rows, stride, halo, prompt):
    if prompt:
        x_ref, g_ref, wup_ref, wconv_ref, wdown_ref, xo_ref, state_ref, gbuf = refs
    else:
        x_ref, g_ref, wup_ref, wconv_ref, wdown_ref, prev_ref, xo_ref, state_ref, gbuf = refs
    d_ff = gbuf.shape[1]
    n_tiles = d_ff // MXU_TILE
    bounds = [MXU_TILE * ((n_tiles * c + FFN_CHUNKS - 1) // FFN_CHUNKS) for c in range(FFN_CHUNKS + 1)]

    if prompt:
        @pl.when(pl.program_id(1) == 0)
        def _():
            gbuf[0:halo, :] = jnp.zeros((halo, d_ff), F32)
    else:
        gbuf[0:halo, :] = prev_ref[...]

    x = x_ref[...]
    xn = _rms(x, g_ref[0:1, :]).astype(BF16)
    y = None
    for c in range(FFN_CHUNKS):
        lo, hi = bounds[c], bounds[c + 1]
        cols = slice(lo, hi)
        gate = _dot(xn, wup_ref[:, lo:hi])
        up = _dot(xn, wup_ref[:, d_ff + lo:d_ff + hi])
        gbuf[halo:halo + rows, cols] = gate
        conv = (wconv_ref[0:1, cols] * gbuf[halo - 2 * stride:halo - 2 * stride + rows, cols]
                + wconv_ref[1:2, cols] * gbuf[halo - stride:halo - stride + rows, cols]
                + wconv_ref[2:3, cols] * gate)
        act = conv * jax.nn.sigmoid(conv) * up
        part = _dot(act.astype(BF16), wdown_ref[lo:hi, :])
        y = part if y is None else y + part
    xo_ref[...] = x + _rms(y, g_ref[1:2, :])

    new_state = gbuf[halo + rows - 2 * stride:halo + rows, :]
    if prompt:
        state_ref[0] = new_state
        gbuf[0:halo, :] = gbuf[rows:rows + halo, :]
    else:
        state_ref[...] = new_state


def _ffn_prompt(x2d, batch, seq, g, wup, wconv, wdown, layer):
    d_model = x2d.shape[1]
    d_ff = wdown.shape[1]
    rows = min(ROW_TILE, seq)
    n_t = seq // rows
    halo = SUBLANES
    row_spec = pl.BlockSpec((rows, d_model), lambda b, j: (b * n_t + j, 0))
    return pl.pallas_call(
        functools.partial(_ffn_kernel, rows=rows, stride=1, halo=halo, prompt=True),
        grid=(batch, n_t),
        in_specs=[row_spec, _const_spec(g.shape), _layer_spec(wup.shape, layer), _const_spec(wconv.shape),
                  _layer_spec(wdown.shape, layer)],
        out_specs=[row_spec, pl.BlockSpec((1, 2, d_ff), lambda b, j: (b, 0, 0))],
        out_shape=[jax.ShapeDtypeStruct(x2d.shape, F32), jax.ShapeDtypeStruct((batch, 2, d_ff), F32)],
        scratch_shapes=[pltpu.VMEM((halo + rows, d_ff), F32)],
        compiler_params=_cparams(2),
        name="ffn_prompt",
    )(x2d, g, wup, wconv, wdown)


def _ffn_sample(x2d, dec_batch, g, wup, wconv, wdown, prev, layer):
    rows = x2d.shape[0]
    d_ff = wdown.shape[1]
    halo = 2 * dec_batch
    args = (x2d, g, wup, wconv, wdown, prev)
    return pl.pallas_call(
        functools.partial(_ffn_kernel, rows=rows, stride=dec_batch, halo=halo, prompt=False),
        grid=(1,),
        in_specs=[_layer_spec(a.shape, layer) if a is wup or a is wdown else _const_spec(a.shape)
                  for a in args],
        out_specs=[_full_spec(x2d.shape), _full_spec(prev.shape)],
        out_shape=[jax.ShapeDtypeStruct(x2d.shape, F32), jax.ShapeDtypeStruct(prev.shape, F32)],
        scratch_shapes=[pltpu.VMEM((halo + rows, d_ff), F32)],
        compiler_params=_cparams(1),
        name="ffn_sample",
    )(*args)


def _mla_proj_kernel(*refs, heads, q_lora, kv_lora, qk_nope, absorbed, attn_scale):
    if absorbed:
        (x_ref, g_ref, win_ref, qn_ref, kvn_ref, wq_ref, wukt_ref, cos_ref, sin_ref,
         ckv_ref, kr_ref, qlat_ref, qrot_ref) = refs
    else:
        (x_ref, g_ref, win_ref, qn_ref, kvn_ref, wq_ref, wk_ref, wvt_ref, cos_ref, sin_ref,
         ckv_ref, kr_ref, q_ref, k_ref, vt_ref) = refs
    rope = kr_ref.shape[-1]
    cos = cos_ref[...]
    sin = sin_ref[...]

    xn = _rms(x_ref[...], g_ref[0:1, :])
    h = _dot(xn.astype(BF16), win_ref[...])
    c_q = _rms(h[:, 0:q_lora], qn_ref[...]) * attn_scale
    c_kv = _rms(h[:, q_lora:q_lora + kv_lora], kvn_ref[...])
    base = q_lora + kv_lora
    low = lax.broadcasted_iota(jnp.int32, (1, LANES), 1) < rope
    kr_terms = h[:, base:base + LANES] * jnp.where(low, cos, sin)
    kr_both = kr_terms + pltpu.roll(kr_terms, LANES // 2, axis=1)
    ckv_ref[...] = c_kv
    kr_ref[...] = kr_both[:, 0:rope]

    q_all = _dot(c_q.astype(BF16), wq_ref[...])
    n_nope = heads * qk_nope
    n_rope = heads * rope
    q_rot = [q_all[:, n_nope + p * LANES:n_nope + (p + 1) * LANES] * cos
             + q_all[:, n_nope + n_rope + p * LANES:n_nope + n_rope + (p + 1) * LANES] * sin
             for p in range(n_rope // LANES)]

    if absorbed:
        for hd in range(heads):
            q_nope = q_all[:, hd * qk_nope:(hd + 1) * qk_nope].astype(BF16)
            qlat_ref[hd] = _dot(q_nope, wukt_ref[hd]).astype(BF16)
        qrot_ref[...] = jnp.concatenate(q_rot, axis=1).astype(BF16)
    else:
        kr_lo = jnp.where(low, kr_both, 0.0)
        kr_hi = jnp.where(low, 0.0, kr_both)
        ckv_b = c_kv.astype(BF16)
        k_nope = _dot(ckv_b, wk_ref[...])
        vt_all = _dot_nt(wvt_ref[...], ckv_b)
        v_head = vt_ref.shape[2]
        per_pair = LANES // rope
        for hd in range(heads):
            q_ref[0, hd] = jnp.concatenate(
                [q_all[:, hd * qk_nope:(hd + 1) * qk_nope], q_rot[hd // per_pair]], axis=1).astype(BF16)
            k_ref[0, hd] = jnp.concatenate(
                [k_nope[:, hd * qk_nope:(hd + 1) * qk_nope], kr_lo if hd % per_pair == 0 else kr_hi],
                axis=1).astype(BF16)
            vt_ref[0, hd] = vt_all[hd * v_head:(hd + 1) * v_head, :].astype(BF16)


def _mla_proj_prompt(x2d, batch, seq, g, win, qn, kvn, wq, wk, wvt, cos, sin, *, heads, rope, attn_scale):
    d_model = x2d.shape[1]
    q_lora, kv_lora = qn.shape[1], kvn.shape[1]
    qk_nope = wk.shape[1] // heads
    v_head = wvt.shape[0] // heads
    rows = min(PROJ_ROW_TILE, seq)
    n_t = seq // rows
    row_map = lambda b, j: (b * n_t + j, 0)
    head_map = lambda b, j: (b, 0, j, 0)
    return pl.pallas_call(
        functools.partial(_mla_proj_kernel, heads=heads, q_lora=q_lora, kv_lora=kv_lora, qk_nope=qk_nope,
                          absorbed=False, attn_scale=attn_scale),
        grid=(batch, n_t),
        in_specs=[pl.BlockSpec((rows, d_model), row_map), _const_spec(g.shape), _const_spec(win.shape),
                  _const_spec(qn.shape), _const_spec(kvn.shape), _const_spec(wq.shape),
                  _const_spec(wk.shape), _const_spec(wvt.shape),
                  pl.BlockSpec((rows, LANES), lambda b, j: (j, 0)),
                  pl.BlockSpec((rows, LANES), lambda b, j: (j, 0))],
        out_specs=[pl.BlockSpec((rows, kv_lora), row_map), pl.BlockSpec((rows, rope), row_map),
                   pl.BlockSpec((1, heads, rows, qk_nope + LANES), head_map),
                   pl.BlockSpec((1, heads, rows, qk_nope + LANES), head_map),
                   pl.BlockSpec((1, heads, v_head, rows), lambda b, j: (b, 0, 0, j))],
        out_shape=[jax.ShapeDtypeStruct((batch * seq, kv_lora), F32),
                   jax.ShapeDtypeStruct((batch * seq, rope), F32),
                   jax.ShapeDtypeStruct((batch, heads, seq, qk_nope + LANES), BF16),
                   jax.ShapeDtypeStruct((batch, heads, seq, qk_nope + LANES), BF16),
                   jax.ShapeDtypeStruct((batch, heads, v_head, seq), BF16)],
        compiler_params=_cparams(2),
        name="mla_proj_prompt",
    )(x2d, g, win, qn, kvn, wq, wk, wvt, cos, sin)


def _mla_proj_sample(x2d, g, win, qn, kvn, wq, wukt, cos, sin, *, heads, rope, attn_scale):
    rows = x2d.shape[0]
    q_lora, kv_lora = qn.shape[1], kvn.shape[1]
    qk_nope = wukt.shape[1]
    args = (x2d, g, win, qn, kvn, wq, wukt, cos, sin)
    out_shapes = [(rows, kv_lora), (rows, rope), (heads, rows, kv_lora), (rows, heads * rope)]
    out_dtypes = [F32, F32, BF16, BF16]
    return pl.pallas_call(
        functools.partial(_mla_proj_kernel, heads=heads, q_lora=q_lora, kv_lora=kv_lora, qk_nope=qk_nope,
                          absorbed=True, attn_scale=attn_scale),
        grid=(1,),
        in_specs=[_const_spec(a.shape) for a in args],
        out_specs=[_full_spec(s) for s in out_shapes],
        out_shape=[jax.ShapeDtypeStruct(s, d) for s, d in zip(out_shapes, out_dtypes)],
        compiler_params=_cparams(1),
        name="mla_proj_sample",
    )(*args)


def _attn_prompt_kernel(q_ref, k_ref, vt_ref, o_ref, vtx, *, seq, q_block, k_tile, heads_per_step):
    v_head = vt_ref.shape[2]
    for h in range(heads_per_step):
        vtx[h, 0:v_head, :] = vt_ref[0, h]
        vtx[h, v_head:v_head + BF16_SUBLANES, :] = jnp.ones((BF16_SUBLANES, seq), BF16)
    units = []
    for h in range(heads_per_step):
        for qc in range(seq // q_block):
            k_end = (qc + 1) * q_block
            for k0 in range(0, k_end, k_tile):
                units.append((h, qc, k0, min(k0 + k_tile, k_end)))

    def scores(unit):
        h, qc, k0, k1 = unit
        return _dot_nt(k_ref[0, h, k0:k1, :], q_ref[0, h, qc * q_block:(qc + 1) * q_block, :])

    pending = [scores(u) for u in units[:ATTN_LOOKAHEAD]]
    m = acc = None
    for n, (h, qc, k0, k1) in enumerate(units):
        s = pending.pop(0)
        if n + ATTN_LOOKAHEAD < len(units):
            pending.append(scores(units[n + ATTN_LOOKAHEAD]))
        on_diagonal = k1 == (qc + 1) * q_block
        if on_diagonal:
            key_pos = k0 + lax.broadcasted_iota(jnp.int32, s.shape, 0)
            qry_pos = qc * q_block + lax.broadcasted_iota(jnp.int32, s.shape, 1)
            s = jnp.where(key_pos <= qry_pos, s, NEG_BIG)
        s_max = jnp.max(s, axis=0, keepdims=True)
        m_new = s_max if k0 == 0 else jnp.maximum(m, s_max)
        pv = _dot(vtx[h, :, k0:k1], jnp.exp2(s - m_new).astype(BF16))
        acc = pv if k0 == 0 else jnp.exp2(m - m_new) * acc + pv
        m = m_new
        if on_diagonal:
            o_t = acc[0:v_head, :] / acc[v_head:v_head + 1, :]
            o_ref[0, qc * q_block:(qc + 1) * q_block, h * v_head:(h + 1) * v_head] = o_t.T.astype(o_ref.dtype)


def _attn_prompt(q, k, vt):
    batch, heads, seq, dq = q.shape
    v_head = vt.shape[2]
    q_block = min(ATTN_Q_BLOCK, seq)
    k_tile = min(ATTN_K_TILE, seq)
    hps = ATTN_HEADS_PER_STEP
    bh_map = lambda b, h: (b, h, 0, 0)
    return pl.pallas_call(
        functools.partial(_attn_prompt_kernel, seq=seq, q_block=q_block, k_tile=k_tile, heads_per_step=hps),
        grid=(batch, heads // hps),
        in_specs=[pl.BlockSpec((1, hps, seq, dq), bh_map), pl.BlockSpec((1, hps, seq, dq), bh_map),
                  pl.BlockSpec((1, hps, v_head, seq), bh_map)],
        out_specs=pl.BlockSpec((1, seq, hps * v_head), lambda b, h: (b, 0, h)),
        out_shape=jax.ShapeDtypeStruct((batch, seq, heads * v_head), BF16),
        scratch_shapes=[pltpu.VMEM((hps, v_head + BF16_SUBLANES, seq), BF16)],
        compiler_params=_cparams(2),
        name="attn_prompt",
    )(q, k, vt)


def _attn_out_kernel(*refs, heads, absorbed):
    if absorbed:
        o_ref, x_ref, g_ref, wuv_ref, wout_ref, xo_ref = refs
        o = jnp.concatenate([_dot(o_ref[hd], wuv_ref[hd]) for hd in range(heads)], axis=1).astype(BF16)
    else:
        o_ref, x_ref, g_ref, wout_ref, xo_ref = refs
        o = o_ref[...]
    y = _dot(o, wout_ref[...])
    xo_ref[...] = x_ref[...] + _rms(y, g_ref[1:2, :])


def _attn_out_prompt(o2d, x2d, g, wout, layer):
    n_rows, d_model = x2d.shape
    rows = min(OUT_ROW_TILE, n_rows)
    row_map = lambda i: (i, 0)
    return pl.pallas_call(
        functools.partial(_attn_out_kernel, heads=0, absorbed=False),
        grid=(n_rows // rows,),
        in_specs=[pl.BlockSpec((rows, o2d.shape[1]), row_map), pl.BlockSpec((rows, d_model), row_map),
                  _const_spec(g.shape), _layer_spec(wout.shape, layer)],
        out_specs=pl.BlockSpec((rows, d_model), row_map),
        out_shape=jax.ShapeDtypeStruct(x2d.shape, F32),
        compiler_params=_cparams(1),
        name="attn_out_prompt",
    )(o2d, x2d, g, wout)


def _attn_out_sample(o_lat, x2d, g, wuv, wout, layer):
    heads = o_lat.shape[0]
    args = (o_lat, x2d, g, wuv, wout)
    return pl.pallas_call(
        functools.partial(_attn_out_kernel, heads=heads, absorbed=True),
        grid=(1,),
        in_specs=[_layer_spec(a.shape, layer) if a is wout else _const_spec(a.shape) for a in args],
        out_specs=_full_spec(x2d.shape),
        out_shape=jax.ShapeDtypeStruct(x2d.shape, F32),
        compiler_params=_cparams(1),
        name="attn_out_sample",
    )(*args)


def _attn_decode_kernel(pt_ref, qlat_ref, qrot_ref, cnew_ref, rnew_ref, ck_hbm, krt_hbm, o_ref,
                        kbuf, rbuf, kb16, sems, *, layer, n_pages, page, n_new, chunk_pages):
    b = pl.program_id(0)
    n_b = pl.num_programs(0)
    slot = lax.rem(b, 2)

    def page_copies(req, sl):
        cps = []
        for j in range(n_pages):
            pg = pt_ref[req * n_pages + j]
            cps.append(pltpu.make_async_copy(ck_hbm.at[layer, pg], kbuf.at[sl, pl.ds(j * page, page), :],
                                             sems.at[0, sl]))
            cps.append(pltpu.make_async_copy(krt_hbm.at[layer, pg], rbuf.at[sl, j], sems.at[1, sl]))
        return cps

    @pl.when(b == 0)
    def _():
        for cp in page_copies(0, 0):
            cp.start()

    @pl.when(b + 1 < n_b)
    def _():
        for cp in page_copies(b + 1, 1 - slot):
            cp.start()

    for cp in page_copies(b, slot):
        cp.wait()

    q = qlat_ref[0]
    qr = qrot_ref[0]
    chunk = chunk_pages * page
    n_chunks = n_pages // chunk_pages

    def scores(c):
        kb16[c * chunk:(c + 1) * chunk, :] = kbuf[slot, c * chunk:(c + 1) * chunk, :].astype(BF16)
        krt = jnp.concatenate([rbuf[slot, c * chunk_pages + j] for j in range(chunk_pages)],
                              axis=1).astype(BF16)
        return _dot_nt(q, kb16[c * chunk:(c + 1) * chunk, :]) + _dot(qr, krt)

    qf = q.astype(F32)
    qrf = qr.astype(F32)
    n_rows = qf.shape[0]
    t_of_row = lax.rem(lax.broadcasted_iota(jnp.int32, (n_rows, 1), 0), n_new)
    s_new = []
    for t in range(n_new):
        st = (jnp.sum(qf * cnew_ref[0, t:t + 1, :], axis=-1, keepdims=True)
              + jnp.sum(qrf * rnew_ref[0, t:t + 1, :], axis=-1, keepdims=True))
        s_new.append(jnp.where(t_of_row >= t, st, NEG_BIG))

    m = s_new[0]
    for st in s_new[1:]:
        m = jnp.maximum(m, st)
    l = jnp.zeros_like(m)
    acc = jnp.zeros((n_rows, kb16.shape[1]), F32)
    for t, st in enumerate(s_new):
        pt = jnp.where(t_of_row >= t, jnp.exp2(st - m), 0.0)
        l = l + pt
        acc = acc + pt * cnew_ref[0, t:t + 1, :]

    s_next = scores(0)
    for c in range(n_chunks):
        s = s_next
        if c + 1 < n_chunks:
            s_next = scores(c + 1)
        m_new = jnp.maximum(m, jnp.max(s, axis=-1, keepdims=True))
        alpha = jnp.exp2(m - m_new)
        p = jnp.exp2(s - m_new)
        l = alpha * l + jnp.sum(p, axis=-1, keepdims=True)
        acc = alpha * acc + _dot(p.astype(BF16), kb16[c * chunk:(c + 1) * chunk, :])
        m = m_new
    o_ref[0] = (acc / l).astype(o_ref.dtype)


def _attn_decode(page_table_flat, qlat, qrot, cnew, rnew, cache_ckv, cache_krt, layer, n_pages):
    dec_batch, n_rows, kv_lora = qlat.shape
    rope = qrot.shape[-1]
    n_new = cnew.shape[1]
    page = cache_ckv.shape[2]
    chunk_pages = max(p for p in range(1, MAX_PAGES_PER_STEP + 1) if n_pages % p == 0)
    per_b = lambda b, pt: (b, 0, 0)
    grid_spec = pltpu.PrefetchScalarGridSpec(
        num_scalar_prefetch=1,
        grid=(dec_batch,),
        in_specs=[pl.BlockSpec((1, n_rows, kv_lora), per_b), pl.BlockSpec((1, n_rows, rope), per_b),
                  pl.BlockSpec((1, n_new, kv_lora), per_b), pl.BlockSpec((1, n_new, rope), per_b),
                  pl.BlockSpec(memory_space=pl.ANY), pl.BlockSpec(memory_space=pl.ANY)],
        out_specs=pl.BlockSpec((1, n_rows, kv_lora), per_b),
        scratch_shapes=[pltpu.VMEM((2, n_pages * page, kv_lora), F32),
                        pltpu.VMEM((2, n_pages, rope, page), F32),
                        pltpu.VMEM((n_pages * page, kv_lora), BF16),
                        pltpu.SemaphoreType.DMA((2, 2))],
    )
    return pl.pallas_call(
        functools.partial(_attn_decode_kernel, layer=layer, n_pages=n_pages, page=page, n_new=n_new,
                          chunk_pages=chunk_pages),
        grid_spec=grid_spec,
        out_shape=jax.ShapeDtypeStruct((dec_batch, n_rows, kv_lora), BF16),
        compiler_params=_cparams(1),
        name="attn_decode",
    )(page_table_flat, qlat, qrot, cnew, rnew, cache_ckv, cache_krt)


def _rope_tables(pos, rope):
    half = rope // 2
    freqs = ROPE_THETA ** (-jnp.arange(half, dtype=F32) / half)
    ang = pos.astype(F32)[:, None] * freqs[None, :]
    cos, sin = jnp.cos(ang), jnp.sin(ang)
    reps = LANES // rope
    return (jnp.concatenate([cos, cos] * reps, axis=1), jnp.concatenate([-sin, sin] * reps, axis=1))


def _swap_halves(w):
    half = w.shape[-1] // 2
    return jnp.concatenate([w[..., half:], w[..., :half]], axis=-1)


def kernel(x_prompt, x_sample, state_pool, state_sconv, cache_ckv, cache_krope, state_ffn, page_table,
           norms, w_in_even, w_pool_map, pool_scale, w_sconv, w_out_even, w_in_odd, q_norm, kv_norm,
           w_uq, w_uk, w_uv, w_out_odd, w_ffn_up, w_ffn_conv, w_ffn_down):
    batch, seq, d_model = x_prompt.shape
    dec_batch, n_new, _ = x_sample.shape
    depth = norms.shape[0]
    n_pages = page_table.shape[1]
    page = cache_ckv.shape[2]
    past_len = n_pages * page
    q_lora, heads, qk_head = w_uq.shape[1:]
    kv_lora, _, qk_nope = w_uk.shape[1:]
    v_head = w_uv.shape[-1]
    rope = qk_head - qk_nope
    d_pool = state_pool.shape[-1]
    d_ff = w_ffn_down.shape[1]
    n_groups = len(POOL_WINDOWS)
    pool_ch = d_pool // n_groups
    attn_scale = float(qk_head) ** -0.5 * LOG2E
    assert rope * 2 == LANES and qk_nope == LANES and d_pool % (n_groups * LANES) == 0
    assert d_ff % MXU_TILE == 0 and dec_batch % SUBLANES == 0 and seq % SUBLANES == 0
    assert heads % ATTN_HEADS_PER_STEP == 0

    xp = x_prompt.reshape(batch * seq, d_model)
    xs = x_sample.transpose(1, 0, 2).reshape(n_new * dec_batch, d_model)
    page_flat = page_table.reshape(-1)
    cache_krt = cache_krope.transpose(0, 1, 3, 2)
    cos_p, sin_p = _rope_tables(jnp.arange(seq), rope)
    cos_s, sin_s = _rope_tables(jnp.repeat(past_len + jnp.arange(n_new), dec_batch), rope)

    win_even = w_in_even.astype(BF16)
    wout_even = w_out_even.astype(BF16)
    wout_odd = w_out_odd.astype(BF16)
    wup_all = w_ffn_up.astype(BF16)
    wdown_all = w_ffn_down.astype(BF16)

    outs = {k: [] for k in ("pool_p", "pool_s", "conv_p", "conv_s", "ckv_p", "kr_p", "ckv_s", "kr_s",
                            "ffn_p", "ffn_s")}
    for i in range(depth):
        g_mix, g_ffn = norms[i, 0:2], norms[i, 2:4]
        if i % 2 == 0:
            e = i // 2
            wmap = jnp.zeros((d_pool, d_pool), F32)
            for gi in range(n_groups):
                wmap = wmap.at[gi * pool_ch:(gi + 1) * pool_ch, gi * pool_ch:(gi + 1) * pool_ch].set(
                    w_pool_map[e, gi])
            wmap = wmap.astype(BF16)
            scale = pool_scale[e].reshape(1, d_pool)
            xp, pool_p, conv_p = _even_prompt(xp, batch, seq, g_mix, win_even, wmap, scale, w_sconv[e],
                                              wout_even, e)
            pool_prev = state_pool[e].transpose(1, 0, 2).reshape(POOL_BUF * dec_batch, d_pool)
            conv_prev = state_sconv[e].transpose(1, 0, 2).reshape(2 * dec_batch, d_pool)
            xs, pool_s, conv_s = _even_sample(xs, dec_batch, g_mix, win_even, wmap, scale, w_sconv[e],
                                              wout_even, pool_prev, conv_prev, e)
            outs["pool_p"].append(pool_p)
            outs["conv_p"].append(conv_p)
            outs["pool_s"].append(pool_s.reshape(POOL_BUF, dec_batch, d_pool).transpose(1, 0, 2))
            outs["conv_s"].append(conv_s.reshape(2, dec_batch, d_pool).transpose(1, 0, 2))
        else:
            o = i // 2
            w_kr = w_in_odd[o][:, q_lora + kv_lora:]
            win = jnp.concatenate([w_in_odd[o], _swap_halves(w_kr)], axis=1).astype(BF16)
            wq_nope = w_uq[o][:, :, :qk_nope].reshape(q_lora, heads * qk_nope)
            wq_rope = w_uq[o][:, :, qk_nope:]
            wq = jnp.concatenate([wq_nope, wq_rope.reshape(q_lora, heads * rope),
                                  _swap_halves(wq_rope).reshape(q_lora, heads * rope)], axis=1).astype(BF16)
            qn = q_norm[o].reshape(1, q_lora)
            kvn = kv_norm[o].reshape(1, kv_lora)
            wk = w_uk[o].reshape(kv_lora, heads * qk_nope).astype(BF16)
            wvt = w_uv[o].transpose(1, 2, 0).reshape(heads * v_head, kv_lora).astype(BF16)
            ckv_p, kr_p, q, k, vt = _mla_proj_prompt(xp, batch, seq, g_mix, win, qn, kvn, wq, wk, wvt,
                                                     cos_p, sin_p, heads=heads, rope=rope,
                                                     attn_scale=attn_scale)
            o_p = _attn_prompt(q, k, vt)
            xp = _attn_out_prompt(o_p.reshape(batch * seq, heads * v_head), xp, g_mix, wout_odd, o)
            outs["ckv_p"].append(ckv_p.reshape(batch, seq, kv_lora))
            outs["kr_p"].append(kr_p.reshape(batch, seq, rope))
            wukt = w_uk[o].transpose(1, 2, 0).astype(BF16)
            wuv = w_uv[o].transpose(1, 0, 2).astype(BF16)
            ckv_s, kr_s, qlat, qrot = _mla_proj_sample(xs, g_mix, win, qn, kvn, wq, wukt, cos_s, sin_s,
                                                       heads=heads, rope=rope, attn_scale=attn_scale)
            ckv_s = ckv_s.reshape(n_new, dec_batch, kv_lora).transpose(1, 0, 2)
            kr_s = kr_s.reshape(n_new, dec_batch, rope).transpose(1, 0, 2)
            qlat = qlat.reshape(heads, n_new, dec_batch, kv_lora).transpose(2, 0, 1, 3).reshape(
                dec_batch, heads * n_new, kv_lora)
            qrot = qrot.reshape(n_new, dec_batch, heads, rope).transpose(1, 2, 0, 3).reshape(
                dec_batch, heads * n_new, rope)
            o_lat = _attn_decode(page_flat, qlat, qrot, ckv_s, kr_s, cache_ckv, cache_krt, o, n_pages)
            o_lat = o_lat.reshape(dec_batch, heads, n_new, kv_lora).transpose(1, 2, 0, 3).reshape(
                heads, n_new * dec_batch, kv_lora)
            xs = _attn_out_sample(o_lat, xs, g_mix, wuv, wout_odd, o)
            outs["ckv_s"].append(ckv_s)
            outs["kr_s"].append(kr_s)
        xp, ffn_p = _ffn_prompt(xp, batch, seq, g_ffn, wup_all, w_ffn_conv[i], wdown_all, i)
        ffn_prev = state_ffn[i].transpose(1, 0, 2).reshape(2 * dec_batch, d_ff)
        xs, ffn_s = _ffn_sample(xs, dec_batch, g_ffn, wup_all, w_ffn_conv[i], wdown_all, ffn_prev, i)
        outs["ffn_p"].append(ffn_p)
        outs["ffn_s"].append(ffn_s.reshape(2, dec_batch, d_ff).transpose(1, 0, 2))

    y_prompt = xp.reshape(batch, seq, d_model)
    y_sample = xs.reshape(n_new, dec_batch, d_model).transpose(1, 0, 2)
    st = {k: jnp.stack(v) for k, v in outs.items()}
    return (y_prompt, y_sample, st["pool_p"], st["pool_s"], st["conv_p"], st["conv_s"],
            st["ckv_p"], st["kr_p"], st["ckv_s"], st["kr_s"], st["ffn_p"], st["ffn_s"])
```

```python
import functools

import jax
import jax.numpy as jnp
from jax import lax
from jax.experimental import pallas as pl
from jax.experimental.pallas import tpu as pltpu

F32 = jnp.float32
BF16 = jnp.bfloat16

NORM_EPS = 1e-6
POOL_WINDOWS = (2, 4, 8, 16)
POOL_BUF = max(POOL_WINDOWS) - 1
ROPE_THETA = 10000.0
LANES = 128
SUBLANES = 8
BF16_SUBLANES = 16
MXU_TILE = 256
LOG2E = 1.4426950408889634
NEG_BIG = -1e30

ROW_TILE = 512
OUT_ROW_TILE = 1024
PROJ_ROW_TILE = 512
ATTN_Q_BLOCK = 256
ATTN_K_TILE = 512
ATTN_HEADS_PER_STEP = 4
ATTN_LOOKAHEAD = 6
FFN_CHUNKS = 2
MAX_PAGES_PER_STEP = 16
DECODE_SLOTS = 3
VMEM_LIMIT = 56 * 1024 * 1024


def _cparams(n_axes):
    return pltpu.CompilerParams(dimension_semantics=("arbitrary",) * n_axes, vmem_limit_bytes=VMEM_LIMIT)


def _const_spec(shape):
    n = len(shape)
    return pl.BlockSpec(shape, lambda *_: (0,) * n, pipeline_mode=pl.Buffered(1))


def _layer_spec(stacked_shape, layer):
    n = len(stacked_shape) - 1
    return pl.BlockSpec((None,) + tuple(stacked_shape[1:]), lambda *_: (layer,) + (0,) * n,
                        pipeline_mode=pl.Buffered(1))


def _full_spec(shape):
    n = len(shape)
    return pl.BlockSpec(shape, lambda *_: (0,) * n)


def _rms(x, g):
    inv = lax.rsqrt(jnp.mean(x * x, axis=-1, keepdims=True) + NORM_EPS)
    return x * inv * g


def _dot(a, b):
    return jnp.dot(a, b, preferred_element_type=F32)


def _dot_nt(a, b):
    return lax.dot_general(a, b, (((1,), (1,)), ((), ())), preferred_element_type=F32)


def _even_kernel(*refs, rows, stride, halo_u, halo_v, prompt):
    if prompt:
        (x_ref, g_ref, win_ref, wmap_ref, scale_ref, wconv_ref, wout_ref,
         xo_ref, pool_ref, conv_ref, ubuf, vbuf) = refs
    else:
        (x_ref, g_ref, win_ref, wmap_ref, scale_ref, wconv_ref, wout_ref, pool_prev_ref, conv_prev_ref,
         xo_ref, pool_ref, conv_ref, ubuf, vbuf) = refs
    d_pool = ubuf.shape[1]
    pool_ch = d_pool // len(POOL_WINDOWS)

    if prompt:
        j = pl.program_id(1)

        @pl.when(j == 0)
        def _():
            ubuf[0:halo_u, :] = jnp.zeros((halo_u, d_pool), F32)
            vbuf[0:halo_v, :] = jnp.zeros((halo_v, d_pool), F32)
    else:
        ubuf[0:halo_u, :] = pool_prev_ref[...]
        vbuf[0:halo_v, :] = conv_prev_ref[...]

    x = x_ref[...]
    xn = _rms(x, g_ref[0:1, :])
    h = _dot(xn.astype(BF16), win_ref[...])
    u = h[:, 0:d_pool]
    gate_b = h[:, d_pool:2 * d_pool]
    gate_c = h[:, 2 * d_pool:3 * d_pool]
    hx = h[:, 3 * d_pool:4 * d_pool]
    ubuf[halo_u:halo_u + rows, :] = u
    vbuf[halo_v:halo_v + rows, :] = gate_c * hx

    if prompt:
        pos = j * rows + lax.broadcasted_iota(jnp.int32, (rows, 1), 0)
    pieces = []
    for g, w in enumerate(POOL_WINDOWS):
        cols = slice(g * pool_ch, (g + 1) * pool_ch)
        cur = ubuf[halo_u:halo_u + rows, cols]
        total = cur
        for s in range(1, w):
            total = total + ubuf[halo_u - s * stride:halo_u - s * stride + rows, cols]
        if prompt:
            count = jnp.minimum(pos + 1, w).astype(F32)
            pieces.append(total / count - cur)
        else:
            pieces.append(total / float(w) - cur)
    d = jnp.concatenate(pieces, axis=1)
    y_pool = _dot(d.astype(BF16), wmap_ref[...]) * scale_ref[...]

    conv = (wconv_ref[0:1, :] * vbuf[halo_v - 2 * stride:halo_v - 2 * stride + rows, :]
            + wconv_ref[1:2, :] * vbuf[halo_v - stride:halo_v - stride + rows, :]
            + wconv_ref[2:3, :] * vbuf[halo_v:halo_v + rows, :])
    y_conv = gate_b * conv
    y = _dot(jnp.concatenate([y_pool, y_conv], axis=1).astype(BF16), wout_ref[...])
    xo_ref[...] = x + _rms(y, g_ref[1:2, :])

    new_pool = ubuf[halo_u + rows - POOL_BUF * stride:halo_u + rows, :]
    new_conv = vbuf[halo_v + rows - 2 * stride:halo_v + rows, :]
    if prompt:
        pool_ref[0] = new_pool
        conv_ref[0] = new_conv
        ubuf[0:halo_u, :] = ubuf[rows:rows + halo_u, :]
        vbuf[0:halo_v, :] = vbuf[rows:rows + halo_v, :]
    else:
        pool_ref[...] = new_pool
        conv_ref[...] = new_conv


def _even_prompt(x2d, batch, seq, g, win, wmap, scale, wconv, wout, layer):
    d_model = x2d.shape[1]
    d_pool = wmap.shape[0]
    rows = min(ROW_TILE, seq)
    n_t = seq // rows
    halo_u, halo_v = 2 * SUBLANES, SUBLANES
    row_spec = pl.BlockSpec((rows, d_model), lambda b, j: (b * n_t + j, 0))
    return pl.pallas_call(
        functools.partial(_even_kernel, rows=rows, stride=1, halo_u=halo_u, halo_v=halo_v, prompt=True),
        grid=(batch, n_t),
        in_specs=[row_spec, _const_spec(g.shape), _layer_spec(win.shape, layer), _const_spec(wmap.shape),
                  _const_spec(scale.shape), _const_spec(wconv.shape), _layer_spec(wout.shape, layer)],
        out_specs=[row_spec,
                   pl.BlockSpec((1, POOL_BUF, d_pool), lambda b, j: (b, 0, 0)),
                   pl.BlockSpec((1, 2, d_pool), lambda b, j: (b, 0, 0))],
        out_shape=[jax.ShapeDtypeStruct(x2d.shape, F32),
                   jax.ShapeDtypeStruct((batch, POOL_BUF, d_pool), F32),
                   jax.ShapeDtypeStruct((batch, 2, d_pool), F32)],
        scratch_shapes=[pltpu.VMEM((halo_u + rows, d_pool), F32), pltpu.VMEM((halo_v + rows, d_pool), F32)],
        compiler_params=_cparams(2),
        name="even_prompt",
    )(x2d, g, win, wmap, scale, wconv, wout)


def _even_sample(x2d, dec_batch, g, win, wmap, scale, wconv, wout, pool_prev, conv_prev, layer):
    rows, d_model = x2d.shape
    d_pool = wmap.shape[0]
    halo_u, halo_v = POOL_BUF * dec_batch, 2 * dec_batch
    args = (x2d, g, win, wmap, scale, wconv, wout, pool_prev, conv_prev)
    return pl.pallas_call(
        functools.partial(_even_kernel, rows=rows, stride=dec_batch, halo_u=halo_u, halo_v=halo_v,
                          prompt=False),
        grid=(1,),
        in_specs=[_layer_spec(a.shape, layer) if a is win or a is wout else _const_spec(a.shape)
                  for a in args],
        out_specs=[_full_spec(x2d.shape), _full_spec(pool_prev.shape), _full_spec(conv_prev.shape)],
        out_shape=[jax.ShapeDtypeStruct(x2d.shape, F32),
                   jax.ShapeDtypeStruct(pool_prev.shape, F32),
                   jax.ShapeDtypeStruct(conv_prev.shape, F32)],
        scratch_shapes=[pltpu.VMEM((halo_u + rows, d_pool), F32), pltpu.VMEM((halo_v + rows, d_pool), F32)],
        compiler_params=_cparams(1),
        name="even_sample",
    )(*args)


def _ffn_kernel(*refs, rows, stride, halo, prompt):
    if prompt:
        x_ref, g_ref, wup_ref, wconv_ref, wdown_ref, xo_ref, state_ref, gbuf = refs
    else:
        x_ref, g_ref, wup_ref, wconv_ref, wdown_ref, prev_ref, xo_ref, state_ref, gbuf = refs
    d_ff = gbuf.shape[1]
    n_tiles = d_ff // MXU_TILE
    bounds = [MXU_TILE * ((n_tiles * c + FFN_CHUNKS - 1) // FFN_CHUNKS) for c in range(FFN_CHUNKS + 1)]

    if prompt:
        @pl.when(pl.program_id(1) == 0)
        def _():
            gbuf[0:halo, :] = jnp.zeros((halo, d_ff), F32)
    else:
        gbuf[0:halo, :] = prev_ref[...]

    x = x_ref[...]
    xn = _rms(x, g_ref[0:1, :]).astype(BF16)
    y = None
    for c in range(FFN_CHUNKS):
        lo, hi = bounds[c], bounds[c + 1]
        cols = slice(lo, hi)
        gate = _dot(xn, wup_ref[:, lo:hi])
        up = _dot(xn, wup_ref[:, d_ff + lo:d_ff + hi])
        gbuf[halo:halo + rows, cols] = gate
        conv = (wconv_ref[0:1, cols] * gbuf[halo - 2 * stride:halo - 2 * stride + rows, cols]
                + wconv_ref[1:2, cols] * gbuf[halo - stride:halo - stride + rows, cols]
                + wconv_ref[2:3, cols] * gate)
        act = conv * jax.nn.sigmoid(conv) * up
        part = _dot(act.astype(BF16), wdown_ref[lo:hi, :])
        y = part if y is None else y + part
    xo_ref[...] = x + _rms(y, g_ref[1:2, :])

    new_state = gbuf[halo + rows - 2 * stride:halo + rows, :]
    if prompt:
        state_ref[0] = new_state
        gbuf[0:halo, :] = gbuf[rows:rows + halo, :]
    else:
        state_ref[...] = new_state


def _ffn_prompt(x2d, batch, seq, g, wup, wconv, wdown, layer):
    d_model = x2d.shape[1]
    d_ff = wdown.shape[1]
    rows = min(ROW_TILE, seq)
    n_t = seq // rows
    halo = SUBLANES
    row_spec = pl.BlockSpec((rows, d_model), lambda b, j: (b * n_t + j, 0))
    return pl.pallas_call(
        functools.partial(_ffn_kernel, rows=rows, stride=1, halo=halo, prompt=True),
        grid=(batch, n_t),
        in_specs=[row_spec, _const_spec(g.shape), _layer_spec(wup.shape, layer), _const_spec(wconv.shape),
                  _layer_spec(wdown.shape, layer)],
        out_specs=[row_spec, pl.BlockSpec((1, 2, d_ff), lambda b, j: (b, 0, 0))],
        out_shape=[jax.ShapeDtypeStruct(x2d.shape, F32), jax.ShapeDtypeStruct((batch, 2, d_ff), F32)],
        scratch_shapes=[pltpu.VMEM((halo + rows, d_ff), F32)],
        compiler_params=_cparams(2),
        name="ffn_prompt",
    )(x2d, g, wup, wconv, wdown)


def _ffn_sample(x2d, dec_batch, g, wup, wconv, wdown, prev, layer):
    rows = x2d.shape[0]
    d_ff = wdown.shape[1]
    halo = 2 * dec_batch
    args = (x2d, g, wup, wconv, wdown, prev)
    return pl.pallas_call(
        functools.partial(_ffn_kernel, rows=rows, stride=dec_batch, halo=halo, prompt=False),
        grid=(1,),
        in_specs=[_layer_spec(a.shape, layer) if a is wup or a is wdown else _const_spec(a.shape)
                  for a in args],
        out_specs=[_full_spec(x2d.shape), _full_spec(prev.shape)],
        out_shape=[jax.ShapeDtypeStruct(x2d.shape, F32), jax.ShapeDtypeStruct(prev.shape, F32)],
        scratch_shapes=[pltpu.VMEM((halo + rows, d_ff), F32)],
        compiler_params=_cparams(1),
        name="ffn_sample",
    )(*args)


def _mla_proj_kernel(*refs, heads, q_lora, kv_lora, qk_nope, absorbed, attn_scale):
    if absorbed:
        (x_ref, g_ref, win_ref, qn_ref, kvn_ref, wq_ref, wukt_ref, cos_ref, sin_ref,
         ckv_ref, kr_ref, qlat_ref, qrot_ref) = refs
    else:
        (x_ref, g_ref, win_ref, qn_ref, kvn_ref, wq_ref, wk_ref, wvt_ref, cos_ref, sin_ref,
         ckv_ref, kr_ref, q_ref, k_ref, vt_ref) = refs
    rope = kr_ref.shape[-1]
    cos = cos_ref[...]
    sin = sin_ref[...]

    xn = _rms(x_ref[...], g_ref[0:1, :])
    h = _dot(xn.astype(BF16), win_ref[...])
    c_q = _rms(h[:, 0:q_lora], qn_ref[...]) * attn_scale
    c_kv = _rms(h[:, q_lora:q_lora + kv_lora], kvn_ref[...])
    base = q_lora + kv_lora
    low = lax.broadcasted_iota(jnp.int32, (1, LANES), 1) < rope
    kr_terms = h[:, base:base + LANES] * jnp.where(low, cos, sin)
    kr_both = kr_terms + pltpu.roll(kr_terms, LANES // 2, axis=1)
    ckv_ref[...] = c_kv
    kr_ref[...] = kr_both[:, 0:rope]

    q_all = _dot(c_q.astype(BF16), wq_ref[...])
    n_nope = heads * qk_nope
    n_rope = heads * rope
    q_rot = [q_all[:, n_nope + p * LANES:n_nope + (p + 1) * LANES] * cos
             + q_all[:, n_nope + n_rope + p * LANES:n_nope + n_rope + (p + 1) * LANES] * sin
             for p in range(n_rope // LANES)]

    if absorbed:
        for hd in range(heads):
            q_nope = q_all[:, hd * qk_nope:(hd + 1) * qk_nope].astype(BF16)
            qlat_ref[hd] = _dot(q_nope, wukt_ref[hd]).astype(BF16)
        qrot_ref[...] = jnp.concatenate(q_rot, axis=1).astype(BF16)
    else:
        kr_lo = jnp.where(low, kr_both, 0.0)
        kr_hi = jnp.where(low, 0.0, kr_both)
        ckv_b = c_kv.astype(BF16)
        k_nope = _dot(ckv_b, wk_ref[...])
        vt_all = _dot_nt(wvt_ref[...], ckv_b)
        v_head = vt_ref.shape[2]
        per_pair = LANES // rope
        for hd in range(heads):
            q_ref[0, hd] = jnp.concatenate(
                [q_all[:, hd * qk_nope:(hd + 1) * qk_nope], q_rot[hd // per_pair]], axis=1).astype(BF16)
            k_ref[0, hd] = jnp.concatenate(
                [k_nope[:, hd * qk_nope:(hd + 1) * qk_nope], kr_lo if hd % per_pair == 0 else kr_hi],
                axis=1).astype(BF16)
            vt_ref[0, hd] = vt_all[hd * v_head:(hd + 1) * v_head, :].astype(BF16)


def _mla_proj_prompt(x2d, batch, seq, g, win, qn, kvn, wq, wk, wvt, cos, sin, *, heads, rope, attn_scale):
    d_model = x2d.shape[1]
    q_lora, kv_lora = qn.shape[1], kvn.shape[1]
    qk_nope = wk.shape[1] // heads
    v_head = wvt.shape[0] // heads
    rows = min(PROJ_ROW_TILE, seq)
    n_t = seq // rows
    row_map = lambda b, j: (b * n_t + j, 0)
    head_map = lambda b, j: (b, 0, j, 0)
    return pl.pallas_call(
        functools.partial(_mla_proj_kernel, heads=heads, q_lora=q_lora, kv_lora=kv_lora, qk_nope=qk_nope,
                          absorbed=False, attn_scale=attn_scale),
        grid=(batch, n_t),
        in_specs=[pl.BlockSpec((rows, d_model), row_map), _const_spec(g.shape), _const_spec(win.shape),
                  _const_spec(qn.shape), _const_spec(kvn.shape), _const_spec(wq.shape),
                  _const_spec(wk.shape), _const_spec(wvt.shape),
                  pl.BlockSpec((rows, LANES), lambda b, j: (j, 0)),
                  pl.BlockSpec((rows, LANES), lambda b, j: (j, 0))],
        out_specs=[pl.BlockSpec((rows, kv_lora), row_map), pl.BlockSpec((rows, rope), row_map),
                   pl.BlockSpec((1, heads, rows, qk_nope + LANES), head_map),
                   pl.BlockSpec((1, heads, rows, qk_nope + LANES), head_map),
                   pl.BlockSpec((1, heads, v_head, rows), lambda b, j: (b, 0, 0, j))],
        out_shape=[jax.ShapeDtypeStruct((batch * seq, kv_lora), F32),
                   jax.ShapeDtypeStruct((batch * seq, rope), F32),
                   jax.ShapeDtypeStruct((batch, heads, seq, qk_nope + LANES), BF16),
                   jax.ShapeDtypeStruct((batch, heads, seq, qk_nope + LANES), BF16),
                   jax.ShapeDtypeStruct((batch, heads, v_head, seq), BF16)],
        compiler_params=_cparams(2),
        name="mla_proj_prompt",
    )(x2d, g, win, qn, kvn, wq, wk, wvt, cos, sin)


def _mla_proj_sample(x2d, g, win, qn, kvn, wq, wukt, cos, sin, *, heads, rope, attn_scale):
    rows = x2d.shape[0]
    q_lora, kv_lora = qn.shape[1], kvn.shape[1]
    qk_nope = wukt.shape[1]
    args = (x2d, g, win, qn, kvn, wq, wukt, cos, sin)
    out_shapes = [(rows, kv_lora), (rows, rope), (heads, rows, kv_lora), (rows, heads * rope)]
    out_dtypes = [F32, F32, BF16, BF16]
    return pl.pallas_call(
        functools.partial(_mla_proj_kernel, heads=heads, q_lora=q_lora, kv_lora=kv_lora, qk_nope=qk_nope,
                          absorbed=True, attn_scale=attn_scale),
        grid=(1,),
        in_specs=[_const_spec(a.shape) for a in args],
        out_specs=[_full_spec(s) for s in out_shapes],
        out_shape=[jax.ShapeDtypeStruct(s, d) for s, d in zip(out_shapes, out_dtypes)],
        compiler_params=_cparams(1),
        name="mla_proj_sample",
    )(*args)


def _attn_prompt_kernel(q_ref, k_ref, vt_ref, o_ref, vtx, *, seq, q_block, k_tile, heads_per_step):
    v_head = vt_ref.shape[2]
    for h in range(heads_per_step):
        vtx[h, 0:v_head, :] = vt_ref[0, h]
        vtx[h, v_head:v_head + BF16_SUBLANES, :] = jnp.ones((BF16_SUBLANES, seq), BF16)
    units = []
    for h in range(heads_per_step):
        for qc in range(seq // q_block):
            k_end = (qc + 1) * q_block
            for k0 in range(0, k_end, k_tile):
                units.append((h, qc, k0, min(k0 + k_tile, k_end)))

    def scores(unit):
        h, qc, k0, k1 = unit
        return _dot_nt(k_ref[0, h, k0:k1, :], q_ref[0, h, qc * q_block:(qc + 1) * q_block, :])

    pending = [scores(u) for u in units[:ATTN_LOOKAHEAD]]
    m = acc = None
    for n, (h, qc, k0, k1) in enumerate(units):
        s = pending.pop(0)
        if n + ATTN_LOOKAHEAD < len(units):
            pending.append(scores(units[n + ATTN_LOOKAHEAD]))
        on_diagonal = k1 == (qc + 1) * q_block
        if on_diagonal:
            key_pos = k0 + lax.broadcasted_iota(jnp.int32, s.shape, 0)
            qry_pos = qc * q_block + lax.broadcasted_iota(jnp.int32, s.shape, 1)
            s = jnp.where(key_pos <= qry_pos, s, NEG_BIG)
        s_max = jnp.max(s, axis=0, keepdims=True)
        m_new = s_max if k0 == 0 else jnp.maximum(m, s_max)
        pv = _dot(vtx[h, :, k0:k1], jnp.exp2(s - m_new).astype(BF16))
        acc = pv if k0 == 0 else jnp.exp2(m - m_new) * acc + pv
        m = m_new
        if on_diagonal:
            o_t = acc[0:v_head, :] / acc[v_head:v_head + 1, :]
            o_ref[0, qc * q_block:(qc + 1) * q_block, h * v_head:(h + 1) * v_head] = o_t.T.astype(o_ref.dtype)


def _attn_prompt(q, k, vt):
    batch, heads, seq, dq = q.shape
    v_head = vt.shape[2]
    q_block = min(ATTN_Q_BLOCK, seq)
    k_tile = min(ATTN_K_TILE, seq)
    hps = ATTN_HEADS_PER_STEP
    bh_map = lambda b, h: (b, h, 0, 0)
    return pl.pallas_call(
        functools.partial(_attn_prompt_kernel, seq=seq, q_block=q_block, k_tile=k_tile, heads_per_step=hps),
        grid=(batch, heads // hps),
        in_specs=[pl.BlockSpec((1, hps, seq, dq), bh_map), pl.BlockSpec((1, hps, seq, dq), bh_map),
                  pl.BlockSpec((1, hps, v_head, seq), bh_map)],
        out_specs=pl.BlockSpec((1, seq, hps * v_head), lambda b, h: (b, 0, h)),
        out_shape=jax.ShapeDtypeStruct((batch, seq, heads * v_head), BF16),
        scratch_shapes=[pltpu.VMEM((hps, v_head + BF16_SUBLANES, seq), BF16)],
        compiler_params=_cparams(2),
        name="attn_prompt",
    )(q, k, vt)


def _attn_out_kernel(*refs, heads, absorbed):
    if absorbed:
        o_ref, x_ref, g_ref, wuv_ref, wout_ref, xo_ref = refs
        o = jnp.concatenate([_dot(o_ref[hd], wuv_ref[hd]) for hd in range(heads)], axis=1).astype(BF16)
    else:
        o_ref, x_ref, g_ref, wout_ref, xo_ref = refs
        o = o_ref[...]
    y = _dot(o, wout_ref[...])
    xo_ref[...] = x_ref[...] + _rms(y, g_ref[1:2, :])


def _attn_out_prompt(o2d, x2d, g, wout, layer):
    n_rows, d_model = x2d.shape
    rows = min(OUT_ROW_TILE, n_rows)
    row_map = lambda i: (i, 0)
    return pl.pallas_call(
        functools.partial(_attn_out_kernel, heads=0, absorbed=False),
        grid=(n_rows // rows,),
        in_specs=[pl.BlockSpec((rows, o2d.shape[1]), row_map), pl.BlockSpec((rows, d_model), row_map),
                  _const_spec(g.shape), _layer_spec(wout.shape, layer)],
        out_specs=pl.BlockSpec((rows, d_model), row_map),
        out_shape=jax.ShapeDtypeStruct(x2d.shape, F32),
        compiler_params=_cparams(1),
        name="attn_out_prompt",
    )(o2d, x2d, g, wout)


def _attn_out_sample(o_lat, x2d, g, wuv, wout, layer):
    heads = o_lat.shape[0]
    args = (o_lat, x2d, g, wuv, wout)
    return pl.pallas_call(
        functools.partial(_attn_out_kernel, heads=heads, absorbed=True),
        grid=(1,),
        in_specs=[_layer_spec(a.shape, layer) if a is wout else _const_spec(a.shape) for a in args],
        out_specs=_full_spec(x2d.shape),
        out_shape=jax.ShapeDtypeStruct(x2d.shape, F32),
        compiler_params=_cparams(1),
        name="attn_out_sample",
    )(*args)


def _attn_decode_kernel(pt_ref, qlat_ref, qrot_ref, cnew_ref, rnew_ref, ck_hbm, krt_hbm, o_ref,
                        kbuf, rbuf, kb16, sems, *, layer, n_req, n_pages, page, n_new, chunk_pages):
    b = pl.program_id(0)
    n_b = pl.num_programs(0)
    ahead = DECODE_SLOTS - 1
    slot = lax.rem(b, DECODE_SLOTS)

    def page_copies(req, sl):
        cps = []
        for j in range(n_pages):
            pg = pt_ref[req * n_pages + j]
            cps.append(pltpu.make_async_copy(ck_hbm.at[layer, pg], kbuf.at[sl, pl.ds(j * page, page), :],
                                             sems.at[0, sl]))
            cps.append(pltpu.make_async_copy(krt_hbm.at[layer, pg], rbuf.at[sl, j], sems.at[1, sl]))
        return cps

    @pl.when(b == 0)
    def _():
        for req in range(min(ahead, n_req)):
            for cp in page_copies(req, req):
                cp.start()

    @pl.when(b + ahead < n_b)
    def _():
        for cp in page_copies(b + ahead, lax.rem(b + ahead, DECODE_SLOTS)):
            cp.start()

    for cp in page_copies(b, slot):
        cp.wait()

    q = qlat_ref[0]
    qr = qrot_ref[0]
    chunk = chunk_pages * page
    n_chunks = n_pages // chunk_pages

    def scores(c):
        kb16[c * chunk:(c + 1) * chunk, :] = kbuf[slot, c * chunk:(c + 1) * chunk, :].astype(BF16)
        krt = jnp.concatenate([rbuf[slot, c * chunk_pages + j] for j in range(chunk_pages)],
                              axis=1).astype(BF16)
        return _dot_nt(q, kb16[c * chunk:(c + 1) * chunk, :]) + _dot(qr, krt)

    qf = q.astype(F32)
    qrf = qr.astype(F32)
    n_rows = qf.shape[0]
    t_of_row = lax.rem(lax.broadcasted_iota(jnp.int32, (n_rows, 1), 0), n_new)
    s_new = []
    for t in range(n_new):
        st = (jnp.sum(qf * cnew_ref[0, t:t + 1, :], axis=-1, keepdims=True)
              + jnp.sum(qrf * rnew_ref[0, t:t + 1, :], axis=-1, keepdims=True))
        s_new.append(jnp.where(t_of_row >= t, st, NEG_BIG))

    m = s_new[0]
    for st in s_new[1:]:
        m = jnp.maximum(m, st)
    l = jnp.zeros_like(m)
    acc = jnp.zeros((n_rows, kb16.shape[1]), F32)
    for t, st in enumerate(s_new):
        pt = jnp.where(t_of_row >= t, jnp.exp2(st - m), 0.0)
        l = l + pt
        acc = acc + pt * cnew_ref[0, t:t + 1, :]

    s_next = scores(0)
    for c in range(n_chunks):
        s = s_next
        if c + 1 < n_chunks:
            s_next = scores(c + 1)
        m_new = jnp.maximum(m, jnp.max(s, axis=-1, keepdims=True))
        alpha = jnp.exp2(m - m_new)
        p = jnp.exp2(s - m_new)
        l = alpha * l + jnp.sum(p, axis=-1, keepdims=True)
        acc = alpha * acc + _dot(p.astype(BF16), kb16[c * chunk:(c + 1) * chunk, :])
        m = m_new
    o_ref[0] = (acc / l).astype(o_ref.dtype)


def _attn_decode(page_table_flat, qlat, qrot, cnew, rnew, cache_ckv, cache_krt, layer, n_pages):
    dec_batch, n_rows, kv_lora = qlat.shape
    rope = qrot.shape[-1]
    n_new = cnew.shape[1]
    page = cache_ckv.shape[2]
    chunk_pages = max(p for p in range(1, MAX_PAGES_PER_STEP + 1) if n_pages % p == 0)
    per_b = lambda b, pt: (b, 0, 0)
    grid_spec = pltpu.PrefetchScalarGridSpec(
        num_scalar_prefetch=1,
        grid=(dec_batch,),
        in_specs=[pl.BlockSpec((1, n_rows, kv_lora), per_b), pl.BlockSpec((1, n_rows, rope), per_b),
                  pl.BlockSpec((1, n_new, kv_lora), per_b), pl.BlockSpec((1, n_new, rope), per_b),
                  pl.BlockSpec(memory_space=pl.ANY), pl.BlockSpec(memory_space=pl.ANY)],
        out_specs=pl.BlockSpec((1, n_rows, kv_lora), per_b),
        scratch_shapes=[pltpu.VMEM((DECODE_SLOTS, n_pages * page, kv_lora), F32),
                        pltpu.VMEM((DECODE_SLOTS, n_pages, rope, page), F32),
                        pltpu.VMEM((n_pages * page, kv_lora), BF16),
                        pltpu.SemaphoreType.DMA((2, DECODE_SLOTS))],
    )
    return pl.pallas_call(
        functools.partial(_attn_decode_kernel, layer=layer, n_req=dec_batch, n_pages=n_pages, page=page,
                          n_new=n_new, chunk_pages=chunk_pages),
        grid_spec=grid_spec,
        out_shape=jax.ShapeDtypeStruct((dec_batch, n_rows, kv_lora), BF16),
        compiler_params=_cparams(1),
        name="attn_decode",
    )(page_table_flat, qlat, qrot, cnew, rnew, cache_ckv, cache_krt)


def _rope_tables(pos, rope):
    half = rope // 2
    freqs = ROPE_THETA ** (-jnp.arange(half, dtype=F32) / half)
    ang = pos.astype(F32)[:, None] * freqs[None, :]
    cos, sin = jnp.cos(ang), jnp.sin(ang)
    reps = LANES // rope
    return (jnp.concatenate([cos, cos] * reps, axis=1), jnp.concatenate([-sin, sin] * reps, axis=1))


def _swap_halves(w):
    half = w.shape[-1] // 2
    return jnp.concatenate([w[..., half:], w[..., :half]], axis=-1)


def kernel(x_prompt, x_sample, state_pool, state_sconv, cache_ckv, cache_krope, state_ffn, page_table,
           norms, w_in_even, w_pool_map, pool_scale, w_sconv, w_out_even, w_in_odd, q_norm, kv_norm,
           w_uq, w_uk, w_uv, w_out_odd, w_ffn_up, w_ffn_conv, w_ffn_down):
    batch, seq, d_model = x_prompt.shape
    dec_batch, n_new, _ = x_sample.shape
    depth = norms.shape[0]
    n_pages = page_table.shape[1]
    page = cache_ckv.shape[2]
    past_len = n_pages * page
    q_lora, heads, qk_head = w_uq.shape[1:]
    kv_lora, _, qk_nope = w_uk.shape[1:]
    v_head = w_uv.shape[-1]
    rope = qk_head - qk_nope
    d_pool = state_pool.shape[-1]
    d_ff = w_ffn_down.shape[1]
    n_groups = len(POOL_WINDOWS)
    pool_ch = d_pool // n_groups
    attn_scale = float(qk_head) ** -0.5 * LOG2E
    assert rope * 2 == LANES and qk_nope == LANES and d_pool % (n_groups * LANES) == 0
    assert d_ff % MXU_TILE == 0 and dec_batch % SUBLANES == 0 and seq % SUBLANES == 0
    assert heads % ATTN_HEADS_PER_STEP == 0

    xp = x_prompt.reshape(batch * seq, d_model)
    xs = x_sample.transpose(1, 0, 2).reshape(n_new * dec_batch, d_model)
    page_flat = page_table.reshape(-1)
    cache_krt = cache_krope.transpose(0, 1, 3, 2)
    cos_p, sin_p = _rope_tables(jnp.arange(seq), rope)
    cos_s, sin_s = _rope_tables(jnp.repeat(past_len + jnp.arange(n_new), dec_batch), rope)

    win_even = w_in_even.astype(BF16)
    wout_even = w_out_even.astype(BF16)
    wout_odd = w_out_odd.astype(BF16)
    wup_all = w_ffn_up.astype(BF16)
    wdown_all = w_ffn_down.astype(BF16)

    outs = {k: [] for k in ("pool_p", "pool_s", "conv_p", "conv_s", "ckv_p", "kr_p", "ckv_s", "kr_s",
                            "ffn_p", "ffn_s")}
    for i in range(depth):
        g_mix, g_ffn = norms[i, 0:2], norms[i, 2:4]
        if i % 2 == 0:
            e = i // 2
            wmap = jnp.zeros((d_pool, d_pool), F32)
            for gi in range(n_groups):
                wmap = wmap.at[gi * pool_ch:(gi + 1) * pool_ch, gi * pool_ch:(gi + 1) * pool_ch].set(
                    w_pool_map[e, gi])
            wmap = wmap.astype(BF16)
            scale = pool_scale[e].reshape(1, d_pool)
            xp, pool_p, conv_p = _even_prompt(xp, batch, seq, g_mix, win_even, wmap, scale, w_sconv[e],
                                              wout_even, e)
            pool_prev = state_pool[e].transpose(1, 0, 2).reshape(POOL_BUF * dec_batch, d_pool)
            conv_prev = state_sconv[e].transpose(1, 0, 2).reshape(2 * dec_batch, d_pool)
            xs, pool_s, conv_s = _even_sample(xs, dec_batch, g_mix, win_even, wmap, scale, w_sconv[e],
                                              wout_even, pool_prev, conv_prev, e)
            outs["pool_p"].append(pool_p)
            outs["conv_p"].append(conv_p)
            outs["pool_s"].append(pool_s.reshape(POOL_BUF, dec_batch, d_pool).transpose(1, 0, 2))
            outs["conv_s"].append(conv_s.reshape(2, dec_batch, d_pool).transpose(1, 0, 2))
        else:
            o = i // 2
            w_kr = w_in_odd[o][:, q_lora + kv_lora:]
            win = jnp.concatenate([w_in_odd[o], _swap_halves(w_kr)], axis=1).astype(BF16)
            wq_nope = w_uq[o][:, :, :qk_nope].reshape(q_lora, heads * qk_nope)
            wq_rope = w_uq[o][:, :, qk_nope:]
            wq = jnp.concatenate([wq_nope, wq_rope.reshape(q_lora, heads * rope),
                                  _swap_halves(wq_rope).reshape(q_lora, heads * rope)], axis=1).astype(BF16)
            qn = q_norm[o].reshape(1, q_lora)
            kvn = kv_norm[o].reshape(1, kv_lora)
            wk = w_uk[o].reshape(kv_lora, heads * qk_nope).astype(BF16)
            wvt = w_uv[o].transpose(1, 2, 0).reshape(heads * v_head, kv_lora).astype(BF16)
            ckv_p, kr_p, q, k, vt = _mla_proj_prompt(xp, batch, seq, g_mix, win, qn, kvn, wq, wk, wvt,
                                                     cos_p, sin_p, heads=heads, rope=rope,
                                                     attn_scale=attn_scale)
            o_p = _attn_prompt(q, k, vt)
            xp = _attn_out_prompt(o_p.reshape(batch * seq, heads * v_head), xp, g_mix, wout_odd, o)
            outs["ckv_p"].append(ckv_p.reshape(batch, seq, kv_lora))
            outs["kr_p"].append(kr_p.reshape(batch, seq, rope))
            wukt = w_uk[o].transpose(1, 2, 0).astype(BF16)
            wuv = w_uv[o].transpose(1, 0, 2).astype(BF16)
            ckv_s, kr_s, qlat, qrot = _mla_proj_sample(xs, g_mix, win, qn, kvn, wq, wukt, cos_s, sin_s,
                                                       heads=heads, rope=rope, attn_scale=attn_scale)
            ckv_s = ckv_s.reshape(n_new, dec_batch, kv_lora).transpose(1, 0, 2)
            kr_s = kr_s.reshape(n_new, dec_batch, rope).transpose(1, 0, 2)
            qlat = qlat.reshape(heads, n_new, dec_batch, kv_lora).transpose(2, 0, 1, 3).reshape(
                dec_batch, heads * n_new, kv_lora)
            qrot = qrot.reshape(n_new, dec_batch, heads, rope).transpose(1, 2, 0, 3).reshape(
                dec_batch, heads * n_new, rope)
            o_lat = _attn_decode(page_flat, qlat, qrot, ckv_s, kr_s, cache_ckv, cache_krt, o, n_pages)
            o_lat = o_lat.reshape(dec_batch, heads, n_new, kv_lora).transpose(1, 2, 0, 3).reshape(
                heads, n_new * dec_batch, kv_lora)
            xs = _attn_out_sample(o_lat, xs, g_mix, wuv, wout_odd, o)
            outs["ckv_s"].append(ckv_s)
            outs["kr_s"].append(kr_s)
        xp, ffn_p = _ffn_prompt(xp, batch, seq, g_ffn, wup_all, w_ffn_conv[i], wdown_all, i)
        ffn_prev = state_ffn[i].transpose(1, 0, 2).reshape(2 * dec_batch, d_ff)
        xs, ffn_s = _ffn_sample(xs, dec_batch, g_ffn, wup_all, w_ffn_conv[i], wdown_all, ffn_prev, i)
        outs["ffn_p"].append(ffn_p)
        outs["ffn_s"].append(ffn_s.reshape(2, dec_batch, d_ff).transpose(1, 0, 2))

    y_prompt = xp.reshape(batch, seq, d_model)
    y_sample = xs.reshape(n_new, dec_batch, d_model).transpose(1, 0, 2)
    st = {k: jnp.stack(v) for k, v in outs.items()}
    return (y_prompt, y_sample, st["pool_p"], st["pool_s"], st["conv_p"], st["conv_s"],
            st["ckv_p"], st["kr_p"], st["ckv_s"], st["kr_s"], st["ffn_p"], st["ffn_s"])
```

```python
import functools

import jax
import jax.numpy as jnp
from jax import lax
from jax.experimental import pallas as pl
from jax.experimental.pallas import tpu as pltpu

F32 = jnp.float32
BF16 = jnp.bfloat16

NORM_EPS = 1e-6
POOL_WINDOWS = (2, 4, 8, 16)
POOL_BUF = max(POOL_WINDOWS) - 1
ROPE_THETA = 10000.0
LANES = 128
SUBLANES = 8
BF16_SUBLANES = 16
MXU_TILE = 256
LOG2E = 1.4426950408889634
NEG_BIG = -1e30

ROW_TILE = 512
OUT_ROW_TILE = 1024
PROJ_ROW_TILE = 512
ATTN_Q_BLOCK = 256
ATTN_K_TILE = 512
ATTN_HEADS_PER_STEP = 4
ATTN_LOOKAHEAD = 6
FFN_CHUNKS = 2
MAX_PAGES_PER_STEP = 32
DECODE_SLOTS = 3
VMEM_LIMIT = 56 * 1024 * 1024


def _cparams(n_axes):
    return pltpu.CompilerParams(dimension_semantics=("arbitrary",) * n_axes, vmem_limit_bytes=VMEM_LIMIT)


def _const_spec(shape):
    n = len(shape)
    return pl.BlockSpec(shape, lambda *_: (0,) * n, pipeline_mode=pl.Buffered(1))


def _layer_spec(stacked_shape, layer):
    n = len(stacked_shape) - 1
    return pl.BlockSpec((None,) + tuple(stacked_shape[1:]), lambda *_: (layer,) + (0,) * n,
                        pipeline_mode=pl.Buffered(1))


def _full_spec(shape):
    n = len(shape)
    return pl.BlockSpec(shape, lambda *_: (0,) * n)


def _rms(x, g):
    inv = lax.rsqrt(jnp.mean(x * x, axis=-1, keepdims=True) + NORM_EPS)
    return x * inv * g


def _dot(a, b):
    return jnp.dot(a, b, preferred_element_type=F32)


def _dot_nt(a, b):
    return lax.dot_general(a, b, (((1,), (1,)), ((), ())), preferred_element_type=F32)


def _even_kernel(*refs, rows, stride, halo_u, halo_v, prompt):
    if prompt:
        (x_ref, g_ref, win_ref, wmap_ref, scale_ref, wconv_ref, wout_ref,
         xo_ref, pool_ref, conv_ref, ubuf, vbuf) = refs
    else:
        (x_ref, g_ref, win_ref, wmap_ref, scale_ref, wconv_ref, wout_ref, pool_prev_ref, conv_prev_ref,
         xo_ref, pool_ref, conv_ref, ubuf, vbuf) = refs
    d_pool = ubuf.shape[1]
    pool_ch = d_pool // len(POOL_WINDOWS)

    if prompt:
        j = pl.program_id(1)

        @pl.when(j == 0)
        def _():
            ubuf[0:halo_u, :] = jnp.zeros((halo_u, d_pool), F32)
            vbuf[0:halo_v, :] = jnp.zeros((halo_v, d_pool), F32)
    else:
        ubuf[0:halo_u, :] = pool_prev_ref[...]
        vbuf[0:halo_v, :] = conv_prev_ref[...]

    x = x_ref[...]
    xn = _rms(x, g_ref[0:1, :])
    h = _dot(xn.astype(BF16), win_ref[...])
    u = h[:, 0:d_pool]
    gate_b = h[:, d_pool:2 * d_pool]
    gate_c = h[:, 2 * d_pool:3 * d_pool]
    hx = h[:, 3 * d_pool:4 * d_pool]
    ubuf[halo_u:halo_u + rows, :] = u
    vbuf[halo_v:halo_v + rows, :] = gate_c * hx

    if prompt:
        pos = j * rows + lax.broadcasted_iota(jnp.int32, (rows, 1), 0)
    pieces = []
    for g, w in enumerate(POOL_WINDOWS):
        cols = slice(g * pool_ch, (g + 1) * pool_ch)
        cur = ubuf[halo_u:halo_u + rows, cols]
        total = cur
        for s in range(1, w):
            total = total + ubuf[halo_u - s * stride:halo_u - s * stride + rows, cols]
        if prompt:
            count = jnp.minimum(pos + 1, w).astype(F32)
            pieces.append(total / count - cur)
        else:
            pieces.append(total / float(w) - cur)
    d = jnp.concatenate(pieces, axis=1)
    y_pool = _dot(d.astype(BF16), wmap_ref[...]) * scale_ref[...]

    conv = (wconv_ref[0:1, :] * vbuf[halo_v - 2 * stride:halo_v - 2 * stride + rows, :]
            + wconv_ref[1:2, :] * vbuf[halo_v - stride:halo_v - stride + rows, :]
            + wconv_ref[2:3, :] * vbuf[halo_v:halo_v + rows, :])
    y_conv = gate_b * conv
    y = _dot(jnp.concatenate([y_pool, y_conv], axis=1).astype(BF16), wout_ref[...])
    xo_ref[...] = x + _rms(y, g_ref[1:2, :])

    new_pool = ubuf[halo_u + rows - POOL_BUF * stride:halo_u + rows, :]
    new_conv = vbuf[halo_v + rows - 2 * stride:halo_v + rows, :]
    if prompt:
        pool_ref[0] = new_pool
        conv_ref[0] = new_conv
        ubuf[0:halo_u, :] = ubuf[rows:rows + halo_u, :]
        vbuf[0:halo_v, :] = vbuf[rows:rows + halo_v, :]
    else:
        pool_ref[...] = new_pool
        conv_ref[...] = new_conv


def _even_prompt(x2d, batch, seq, g, win, wmap, scale, wconv, wout, layer):
    d_model = x2d.shape[1]
    d_pool = wmap.shape[0]
    rows = min(ROW_TILE, seq)
    n_t = seq // rows
    halo_u, halo_v = 2 * SUBLANES, SUBLANES
    row_spec = pl.BlockSpec((rows, d_model), lambda b, j: (b * n_t + j, 0))
    return pl.pallas_call(
        functools.partial(_even_kernel, rows=rows, stride=1, halo_u=halo_u, halo_v=halo_v, prompt=True),
        grid=(batch, n_t),
        in_specs=[row_spec, _const_spec(g.shape), _layer_spec(win.shape, layer), _const_spec(wmap.shape),
                  _const_spec(scale.shape), _const_spec(wconv.shape), _layer_spec(wout.shape, layer)],
        out_specs=[row_spec,
                   pl.BlockSpec((1, POOL_BUF, d_pool), lambda b, j: (b, 0, 0)),
                   pl.BlockSpec((1, 2, d_pool), lambda b, j: (b, 0, 0))],
        out_shape=[jax.ShapeDtypeStruct(x2d.shape, F32),
                   jax.ShapeDtypeStruct((batch, POOL_BUF, d_pool), F32),
                   jax.ShapeDtypeStruct((batch, 2, d_pool), F32)],
        scratch_shapes=[pltpu.VMEM((halo_u + rows, d_pool), F32), pltpu.VMEM((halo_v + rows, d_pool), F32)],
        compiler_params=_cparams(2),
        name="even_prompt",
    )(x2d, g, win, wmap, scale, wconv, wout)


def _even_sample(x2d, dec_batch, g, win, wmap, scale, wconv, wout, pool_prev, conv_prev, layer):
    rows, d_model = x2d.shape
    d_pool = wmap.shape[0]
    halo_u, halo_v = POOL_BUF * dec_batch, 2 * dec_batch
    args = (x2d, g, win, wmap, scale, wconv, wout, pool_prev, conv_prev)
    return pl.pallas_call(
        functools.partial(_even_kernel, rows=rows, stride=dec_batch, halo_u=halo_u, halo_v=halo_v,
                          prompt=False),
        grid=(1,),
        in_specs=[_layer_spec(a.shape, layer) if a is win or a is wout else _const_spec(a.shape)
                  for a in args],
        out_specs=[_full_spec(x2d.shape), _full_spec(pool_prev.shape), _full_spec(conv_prev.shape)],
        out_shape=[jax.ShapeDtypeStruct(x2d.shape, F32),
                   jax.ShapeDtypeStruct(pool_prev.shape, F32),
                   jax.ShapeDtypeStruct(conv_prev.shape, F32)],
        scratch_shapes=[pltpu.VMEM((halo_u + rows, d_pool), F32), pltpu.VMEM((halo_v + rows, d_pool), F32)],
        compiler_params=_cparams(1),
        name="even_sample",
    )(*args)


def _ffn_kernel(*refs, rows, stride, halo, prompt):
    if prompt:
        x_ref, g_ref, wup_ref, wconv_ref, wdown_ref, xo_ref, state_ref, gbuf = refs
    else:
        x_ref, g_ref, wup_ref, wconv_ref, wdown_ref, prev_ref, xo_ref, state_ref, gbuf = refs
    d_ff = gbuf.shape[1]
    n_tiles = d_ff // MXU_TILE
    bounds = [MXU_TILE * ((n_tiles * c + FFN_CHUNKS - 1) // FFN_CHUNKS) for c in range(FFN_CHUNKS + 1)]

    if prompt:
        @pl.when(pl.program_id(1) == 0)
        def _():
            gbuf[0:halo, :] = jnp.zeros((halo, d_ff), F32)
    else:
        gbuf[0:halo, :] = prev_ref[...]

    x = x_ref[...]
    xn = _rms(x, g_ref[0:1, :]).astype(BF16)
    y = None
    for c in range(FFN_CHUNKS):
        lo, hi = bounds[c], bounds[c + 1]
        cols = slice(lo, hi)
        gate = _dot(xn, wup_ref[:, lo:hi])
        up = _dot(xn, wup_ref[:, d_ff + lo:d_ff + hi])
        gbuf[halo:halo + rows, cols] = gate
        conv = (wconv_ref[0:1, cols] * gbuf[halo - 2 * stride:halo - 2 * stride + rows, cols]
                + wconv_ref[1:2, cols] * gbuf[halo - stride:halo - stride + rows, cols]
                + wconv_ref[2:3, cols] * gate)
        act = conv * jax.nn.sigmoid(conv) * up
        part = _dot(act.astype(BF16), wdown_ref[lo:hi, :])
        y = part if y is None else y + part
    xo_ref[...] = x + _rms(y, g_ref[1:2, :])

    new_state = gbuf[halo + rows - 2 * stride:halo + rows, :]
    if prompt:
        state_ref[0] = new_state
        gbuf[0:halo, :] = gbuf[rows:rows + halo, :]
    else:
        state_ref[...] = new_state


def _ffn_prompt(x2d, batch, seq, g, wup, wconv, wdown, layer):
    d_model = x2d.shape[1]
    d_ff = wdown.shape[1]
    rows = min(ROW_TILE, seq)
    n_t = seq // rows
    halo = SUBLANES
    row_spec = pl.BlockSpec((rows, d_model), lambda b, j: (b * n_t + j, 0))
    return pl.pallas_call(
        functools.partial(_ffn_kernel, rows=rows, stride=1, halo=halo, prompt=True),
        grid=(batch, n_t),
        in_specs=[row_spec, _const_spec(g.shape), _layer_spec(wup.shape, layer), _const_spec(wconv.shape),
                  _layer_spec(wdown.shape, layer)],
        out_specs=[row_spec, pl.BlockSpec((1, 2, d_ff), lambda b, j: (b, 0, 0))],
        out_shape=[jax.ShapeDtypeStruct(x2d.shape, F32), jax.ShapeDtypeStruct((batch, 2, d_ff), F32)],
        scratch_shapes=[pltpu.VMEM((halo + rows, d_ff), F32)],
        compiler_params=_cparams(2),
        name="ffn_prompt",
    )(x2d, g, wup, wconv, wdown)


def _ffn_sample(x2d, dec_batch, g, wup, wconv, wdown, prev, layer):
    rows = x2d.shape[0]
    d_ff = wdown.shape[1]
    halo = 2 * dec_batch
    args = (x2d, g, wup, wconv, wdown, prev)
    return pl.pallas_call(
        functools.partial(_ffn_kernel, rows=rows, stride=dec_batch, halo=halo, prompt=False),
        grid=(1,),
        in_specs=[_layer_spec(a.shape, layer) if a is wup or a is wdown else _const_spec(a.shape)
                  for a in args],
        out_specs=[_full_spec(x2d.shape), _full_spec(prev.shape)],
        out_shape=[jax.ShapeDtypeStruct(x2d.shape, F32), jax.ShapeDtypeStruct(prev.shape, F32)],
        scratch_shapes=[pltpu.VMEM((halo + rows, d_ff), F32)],
        compiler_params=_cparams(1),
        name="ffn_sample",
    )(*args)


def _mla_proj_kernel(*refs, heads, q_lora, kv_lora, qk_nope, absorbed, attn_scale):
    if absorbed:
        (x_ref, g_ref, win_ref, qn_ref, kvn_ref, wq_ref, wukt_ref, cos_ref, sin_ref,
         ckv_ref, kr_ref, qlat_ref, qrot_ref) = refs
    else:
        (x_ref, g_ref, win_ref, qn_ref, kvn_ref, wq_ref, wk_ref, wvt_ref, cos_ref, sin_ref,
         ckv_ref, kr_ref, q_ref, k_ref, vt_ref) = refs
    rope = kr_ref.shape[-1]
    cos = cos_ref[...]
    sin = sin_ref[...]

    xn = _rms(x_ref[...], g_ref[0:1, :])
    h = _dot(xn.astype(BF16), win_ref[...])
    c_q = _rms(h[:, 0:q_lora], qn_ref[...]) * attn_scale
    c_kv = _rms(h[:, q_lora:q_lora + kv_lora], kvn_ref[...])
    base = q_lora + kv_lora
    low = lax.broadcasted_iota(jnp.int32, (1, LANES), 1) < rope
    kr_terms = h[:, base:base + LANES] * jnp.where(low, cos, sin)
    kr_both = kr_terms + pltpu.roll(kr_terms, LANES // 2, axis=1)
    ckv_ref[...] = c_kv
    kr_ref[...] = kr_both[:, 0:rope]

    q_all = _dot(c_q.astype(BF16), wq_ref[...])
    n_nope = heads * qk_nope
    n_rope = heads * rope
    q_rot = [q_all[:, n_nope + p * LANES:n_nope + (p + 1) * LANES] * cos
             + q_all[:, n_nope + n_rope + p * LANES:n_nope + n_rope + (p + 1) * LANES] * sin
             for p in range(n_rope // LANES)]

    if absorbed:
        for hd in range(heads):
            q_nope = q_all[:, hd * qk_nope:(hd + 1) * qk_nope].astype(BF16)
            qlat_ref[hd] = _dot(q_nope, wukt_ref[hd]).astype(BF16)
        qrot_ref[...] = jnp.concatenate(q_rot, axis=1).astype(BF16)
    else:
        kr_lo = jnp.where(low, kr_both, 0.0)
        kr_hi = jnp.where(low, 0.0, kr_both)
        ckv_b = c_kv.astype(BF16)
        k_nope = _dot(ckv_b, wk_ref[...])
        vt_all = _dot_nt(wvt_ref[...], ckv_b)
        v_head = vt_ref.shape[2]
        per_pair = LANES // rope
        for hd in range(heads):
            q_ref[0, hd] = jnp.concatenate(
                [q_all[:, hd * qk_nope:(hd + 1) * qk_nope], q_rot[hd // per_pair]], axis=1).astype(BF16)
            k_ref[0, hd] = jnp.concatenate(
                [k_nope[:, hd * qk_nope:(hd + 1) * qk_nope], kr_lo if hd % per_pair == 0 else kr_hi],
                axis=1).astype(BF16)
            vt_ref[0, hd] = vt_all[hd * v_head:(hd + 1) * v_head, :].astype(BF16)


def _mla_proj_prompt(x2d, batch, seq, g, win, qn, kvn, wq, wk, wvt, cos, sin, *, heads, rope, attn_scale):
    d_model = x2d.shape[1]
    q_lora, kv_lora = qn.shape[1], kvn.shape[1]
    qk_nope = wk.shape[1] // heads
    v_head = wvt.shape[0] // heads
    rows = min(PROJ_ROW_TILE, seq)
    n_t = seq // rows
    row_map = lambda b, j: (b * n_t + j, 0)
    head_map = lambda b, j: (b, 0, j, 0)
    return pl.pallas_call(
        functools.partial(_mla_proj_kernel, heads=heads, q_lora=q_lora, kv_lora=kv_lora, qk_nope=qk_nope,
                          absorbed=False, attn_scale=attn_scale),
        grid=(batch, n_t),
        in_specs=[pl.BlockSpec((rows, d_model), row_map), _const_spec(g.shape), _const_spec(win.shape),
                  _const_spec(qn.shape), _const_spec(kvn.shape), _const_spec(wq.shape),
                  _const_spec(wk.shape), _const_spec(wvt.shape),
                  pl.BlockSpec((rows, LANES), lambda b, j: (j, 0)),
                  pl.BlockSpec((rows, LANES), lambda b, j: (j, 0))],
        out_specs=[pl.BlockSpec((rows, kv_lora), row_map), pl.BlockSpec((rows, rope), row_map),
                   pl.BlockSpec((1, heads, rows, qk_nope + LANES), head_map),
                   pl.BlockSpec((1, heads, rows, qk_nope + LANES), head_map),
                   pl.BlockSpec((1, heads, v_head, rows), lambda b, j: (b, 0, 0, j))],
        out_shape=[jax.ShapeDtypeStruct((batch * seq, kv_lora), F32),
                   jax.ShapeDtypeStruct((batch * seq, rope), F32),
                   jax.ShapeDtypeStruct((batch, heads, seq, qk_nope + LANES), BF16),
                   jax.ShapeDtypeStruct((batch, heads, seq, qk_nope + LANES), BF16),
                   jax.ShapeDtypeStruct((batch, heads, v_head, seq), BF16)],
        compiler_params=_cparams(2),
        name="mla_proj_prompt",
    )(x2d, g, win, qn, kvn, wq, wk, wvt, cos, sin)


def _mla_proj_sample(x2d, g, win, qn, kvn, wq, wukt, cos, sin, *, heads, rope, attn_scale):
    rows = x2d.shape[0]
    q_lora, kv_lora = qn.shape[1], kvn.shape[1]
    qk_nope = wukt.shape[1]
    args = (x2d, g, win, qn, kvn, wq, wukt, cos, sin)
    out_shapes = [(rows, kv_lora), (rows, rope), (heads, rows, kv_lora), (rows, heads * rope)]
    out_dtypes = [F32, F32, BF16, BF16]
    return pl.pallas_call(
        functools.partial(_mla_proj_kernel, heads=heads, q_lora=q_lora, kv_lora=kv_lora, qk_nope=qk_nope,
                          absorbed=True, attn_scale=attn_scale),
        grid=(1,),
        in_specs=[_const_spec(a.shape) for a in args],
        out_specs=[_full_spec(s) for s in out_shapes],
        out_shape=[jax.ShapeDtypeStruct(s, d) for s, d in zip(out_shapes, out_dtypes)],
        compiler_params=_cparams(1),
        name="mla_proj_sample",
    )(*args)


def _attn_prompt_kernel(q_ref, k_ref, vt_ref, o_ref, vtx, *, seq, q_block, k_tile, heads_per_step):
    v_head = vt_ref.shape[2]
    for h in range(heads_per_step):
        vtx[h, 0:v_head, :] = vt_ref[0, h]
        vtx[h, v_head:v_head + BF16_SUBLANES, :] = jnp.ones((BF16_SUBLANES, seq), BF16)
    units = []
    for h in range(heads_per_step):
        for qc in range(seq // q_block):
            k_end = (qc + 1) * q_block
            for k0 in range(0, k_end, k_tile):
                units.append((h, qc, k0, min(k0 + k_tile, k_end)))

    def scores(unit):
        h, qc, k0, k1 = unit
        return _dot_nt(k_ref[0, h, k0:k1, :], q_ref[0, h, qc * q_block:(qc + 1) * q_block, :])

    pending = [scores(u) for u in units[:ATTN_LOOKAHEAD]]
    m = acc = None
    for n, (h, qc, k0, k1) in enumerate(units):
        s = pending.pop(0)
        if n + ATTN_LOOKAHEAD < len(units):
            pending.append(scores(units[n + ATTN_LOOKAHEAD]))
        on_diagonal = k1 == (qc + 1) * q_block
        if on_diagonal:
            key_pos = k0 + lax.broadcasted_iota(jnp.int32, s.shape, 0)
            qry_pos = qc * q_block + lax.broadcasted_iota(jnp.int32, s.shape, 1)
            s = jnp.where(key_pos <= qry_pos, s, NEG_BIG)
        s_max = jnp.max(s, axis=0, keepdims=True)
        m_new = s_max if k0 == 0 else jnp.maximum(m, s_max)
        pv = _dot(vtx[h, :, k0:k1], jnp.exp2(s - m_new).astype(BF16))
        acc = pv if k0 == 0 else jnp.exp2(m - m_new) * acc + pv
        m = m_new
        if on_diagonal:
            o_t = acc[0:v_head, :] / acc[v_head:v_head + 1, :]
            o_ref[0, qc * q_block:(qc + 1) * q_block, h * v_head:(h + 1) * v_head] = o_t.T.astype(o_ref.dtype)


def _attn_prompt(q, k, vt):
    batch, heads, seq, dq = q.shape
    v_head = vt.shape[2]
    q_block = min(ATTN_Q_BLOCK, seq)
    k_tile = min(ATTN_K_TILE, seq)
    hps = ATTN_HEADS_PER_STEP
    bh_map = lambda b, h: (b, h, 0, 0)
    return pl.pallas_call(
        functools.partial(_attn_prompt_kernel, seq=seq, q_block=q_block, k_tile=k_tile, heads_per_step=hps),
        grid=(batch, heads // hps),
        in_specs=[pl.BlockSpec((1, hps, seq, dq), bh_map), pl.BlockSpec((1, hps, seq, dq), bh_map),
                  pl.BlockSpec((1, hps, v_head, seq), bh_map)],
        out_specs=pl.BlockSpec((1, seq, hps * v_head), lambda b, h: (b, 0, h)),
        out_shape=jax.ShapeDtypeStruct((batch, seq, heads * v_head), BF16),
        scratch_shapes=[pltpu.VMEM((hps, v_head + BF16_SUBLANES, seq), BF16)],
        compiler_params=_cparams(2),
        name="attn_prompt",
    )(q, k, vt)


def _attn_out_kernel(*refs, heads, absorbed):
    if absorbed:
        o_ref, x_ref, g_ref, wuv_ref, wout_ref, xo_ref = refs
        o = jnp.concatenate([_dot(o_ref[hd], wuv_ref[hd]) for hd in range(heads)], axis=1).astype(BF16)
    else:
        o_ref, x_ref, g_ref, wout_ref, xo_ref = refs
        o = o_ref[...]
    y = _dot(o, wout_ref[...])
    xo_ref[...] = x_ref[...] + _rms(y, g_ref[1:2, :])


def _attn_out_prompt(o2d, x2d, g, wout, layer):
    n_rows, d_model = x2d.shape
    rows = min(OUT_ROW_TILE, n_rows)
    row_map = lambda i: (i, 0)
    return pl.pallas_call(
        functools.partial(_attn_out_kernel, heads=0, absorbed=False),
        grid=(n_rows // rows,),
        in_specs=[pl.BlockSpec((rows, o2d.shape[1]), row_map), pl.BlockSpec((rows, d_model), row_map),
                  _const_spec(g.shape), _layer_spec(wout.shape, layer)],
        out_specs=pl.BlockSpec((rows, d_model), row_map),
        out_shape=jax.ShapeDtypeStruct(x2d.shape, F32),
        compiler_params=_cparams(1),
        name="attn_out_prompt",
    )(o2d, x2d, g, wout)


def _attn_out_sample(o_lat, x2d, g, wuv, wout, layer):
    heads = o_lat.shape[0]
    args = (o_lat, x2d, g, wuv, wout)
    return pl.pallas_call(
        functools.partial(_attn_out_kernel, heads=heads, absorbed=True),
        grid=(1,),
        in_specs=[_layer_spec(a.shape, layer) if a is wout else _const_spec(a.shape) for a in args],
        out_specs=_full_spec(x2d.shape),
        out_shape=jax.ShapeDtypeStruct(x2d.shape, F32),
        compiler_params=_cparams(1),
        name="attn_out_sample",
    )(*args)


def _attn_decode_kernel(pt_ref, qlat_ref, qrot_ref, cnew_ref, rnew_ref, ck_hbm, krt_hbm, o_ref,
                        kbuf, rbuf, kb16, sems, *, layer, n_req, n_pages, page, n_new, chunk_pages):
    b = pl.program_id(0)
    n_b = pl.num_programs(0)
    ahead = DECODE_SLOTS - 1
    slot = lax.rem(b, DECODE_SLOTS)

    def page_copies(req, sl):
        cps = []
        for j in range(n_pages):
            pg = pt_ref[req * n_pages + j]
            cps.append(pltpu.make_async_copy(ck_hbm.at[layer, pg], kbuf.at[sl, pl.ds(j * page, page), :],
                                             sems.at[0, sl]))
            cps.append(pltpu.make_async_copy(krt_hbm.at[layer, pg], rbuf.at[sl, j], sems.at[1, sl]))
        return cps

    @pl.when(b == 0)
    def _():
        for req in range(min(ahead, n_req)):
            for i, cp in enumerate(page_copies(req, req)):
                cp.start(priority=(i // 2) % 2)

    @pl.when(b + ahead < n_b)
    def _():
        for i, cp in enumerate(page_copies(b + ahead, lax.rem(b + ahead, DECODE_SLOTS))):
            cp.start(priority=(i // 2) % 2)

    for cp in page_copies(b, slot):
        cp.wait()

    q = qlat_ref[0]
    qr = qrot_ref[0]
    chunk = chunk_pages * page
    n_chunks = n_pages // chunk_pages

    def scores(c):
        kb16[c * chunk:(c + 1) * chunk, :] = kbuf[slot, c * chunk:(c + 1) * chunk, :].astype(BF16)
        krt = jnp.concatenate([rbuf[slot, c * chunk_pages + j] for j in range(chunk_pages)],
                              axis=1).astype(BF16)
        return _dot_nt(q, kb16[c * chunk:(c + 1) * chunk, :]) + _dot(qr, krt)

    qf = q.astype(F32)
    qrf = qr.astype(F32)
    n_rows = qf.shape[0]
    t_of_row = lax.rem(lax.broadcasted_iota(jnp.int32, (n_rows, 1), 0), n_new)
    s_new = []
    for t in range(n_new):
        st = (jnp.sum(qf * cnew_ref[0, t:t + 1, :], axis=-1, keepdims=True)
              + jnp.sum(qrf * rnew_ref[0, t:t + 1, :], axis=-1, keepdims=True))
        s_new.append(jnp.where(t_of_row >= t, st, NEG_BIG))

    m = s_new[0]
    for st in s_new[1:]:
        m = jnp.maximum(m, st)
    l = jnp.zeros_like(m)
    acc = jnp.zeros((n_rows, kb16.shape[1]), F32)
    for t, st in enumerate(s_new):
        pt = jnp.where(t_of_row >= t, jnp.exp2(st - m), 0.0)
        l = l + pt
        acc = acc + pt * cnew_ref[0, t:t + 1, :]

    s_next = scores(0)
    for c in range(n_chunks):
        s = s_next
        if c + 1 < n_chunks:
            s_next = scores(c + 1)
        m_new = jnp.maximum(m, jnp.max(s, axis=-1, keepdims=True))
        alpha = jnp.exp2(m - m_new)
        p = jnp.exp2(s - m_new)
        l = alpha * l + jnp.sum(p, axis=-1, keepdims=True)
        acc = alpha * acc + _dot(p.astype(BF16), kb16[c * chunk:(c + 1) * chunk, :])
        m = m_new
    o_ref[0] = (acc / l).astype(o_ref.dtype)


def _attn_decode(page_table_flat, qlat, qrot, cnew, rnew, cache_ckv, cache_krt, layer, n_pages):
    dec_batch, n_rows, kv_lora = qlat.shape
    rope = qrot.shape[-1]
    n_new = cnew.shape[1]
    page = cache_ckv.shape[2]
    chunk_pages = max(p for p in range(1, MAX_PAGES_PER_STEP + 1) if n_pages % p == 0)
    per_b = lambda b, pt: (b, 0, 0)
    grid_spec = pltpu.PrefetchScalarGridSpec(
        num_scalar_prefetch=1,
        grid=(dec_batch,),
        in_specs=[pl.BlockSpec((1, n_rows, kv_lora), per_b), pl.BlockSpec((1, n_rows, rope), per_b),
                  pl.BlockSpec((1, n_new, kv_lora), per_b), pl.BlockSpec((1, n_new, rope), per_b),
                  pl.BlockSpec(memory_space=pl.ANY), pl.BlockSpec(memory_space=pl.ANY)],
        out_specs=pl.BlockSpec((1, n_rows, kv_lora), per_b),
        scratch_shapes=[pltpu.VMEM((DECODE_SLOTS, n_pages * page, kv_lora), F32),
                        pltpu.VMEM((DECODE_SLOTS, n_pages, rope, page), F32),
                        pltpu.VMEM((n_pages * page, kv_lora), BF16),
                        pltpu.SemaphoreType.DMA((2, DECODE_SLOTS))],
    )
    return pl.pallas_call(
        functools.partial(_attn_decode_kernel, layer=layer, n_req=dec_batch, n_pages=n_pages, page=page,
                          n_new=n_new, chunk_pages=chunk_pages),
        grid_spec=grid_spec,
        out_shape=jax.ShapeDtypeStruct((dec_batch, n_rows, kv_lora), BF16),
        compiler_params=_cparams(1),
        name="attn_decode",
    )(page_table_flat, qlat, qrot, cnew, rnew, cache_ckv, cache_krt)


def _rope_tables(pos, rope):
    half = rope // 2
    freqs = ROPE_THETA ** (-jnp.arange(half, dtype=F32) / half)
    ang = pos.astype(F32)[:, None] * freqs[None, :]
    cos, sin = jnp.cos(ang), jnp.sin(ang)
    reps = LANES // rope
    return (jnp.concatenate([cos, cos] * reps, axis=1), jnp.concatenate([-sin, sin] * reps, axis=1))


def _swap_halves(w):
    half = w.shape[-1] // 2
    return jnp.concatenate([w[..., half:], w[..., :half]], axis=-1)


def kernel(x_prompt, x_sample, state_pool, state_sconv, cache_ckv, cache_krope, state_ffn, page_table,
           norms, w_in_even, w_pool_map, pool_scale, w_sconv, w_out_even, w_in_odd, q_norm, kv_norm,
           w_uq, w_uk, w_uv, w_out_odd, w_ffn_up, w_ffn_conv, w_ffn_down):
    batch, seq, d_model = x_prompt.shape
    dec_batch, n_new, _ = x_sample.shape
    depth = norms.shape[0]
    n_pages = page_table.shape[1]
    page = cache_ckv.shape[2]
    past_len = n_pages * page
    q_lora, heads, qk_head = w_uq.shape[1:]
    kv_lora, _, qk_nope = w_uk.shape[1:]
    v_head = w_uv.shape[-1]
    rope = qk_head - qk_nope
    d_pool = state_pool.shape[-1]
    d_ff = w_ffn_down.shape[1]
    n_groups = len(POOL_WINDOWS)
    pool_ch = d_pool // n_groups
    attn_scale = float(qk_head) ** -0.5 * LOG2E
    assert rope * 2 == LANES and qk_nope == LANES and d_pool % (n_groups * LANES) == 0
    assert d_ff % MXU_TILE == 0 and dec_batch % SUBLANES == 0 and seq % SUBLANES == 0
    assert heads % ATTN_HEADS_PER_STEP == 0

    xp = x_prompt.reshape(batch * seq, d_model)
    xs = x_sample.transpose(1, 0, 2).reshape(n_new * dec_batch, d_model)
    page_flat = page_table.reshape(-1)
    cache_krt = cache_krope.transpose(0, 1, 3, 2)
    cos_p, sin_p = _rope_tables(jnp.arange(seq), rope)
    cos_s, sin_s = _rope_tables(jnp.repeat(past_len + jnp.arange(n_new), dec_batch), rope)

    win_even = w_in_even.astype(BF16)
    wout_even = w_out_even.astype(BF16)
    wout_odd = w_out_odd.astype(BF16)
    wup_all = w_ffn_up.astype(BF16)
    wdown_all = w_ffn_down.astype(BF16)

    outs = {k: [] for k in ("pool_p", "pool_s", "conv_p", "conv_s", "ckv_p", "kr_p", "ckv_s", "kr_s",
                            "ffn_p", "ffn_s")}
    for i in range(depth):
        g_mix, g_ffn = norms[i, 0:2], norms[i, 2:4]
        if i % 2 == 0:
            e = i // 2
            wmap = jnp.zeros((d_pool, d_pool), F32)
            for gi in range(n_groups):
                wmap = wmap.at[gi * pool_ch:(gi + 1) * pool_ch, gi * pool_ch:(gi + 1) * pool_ch].set(
                    w_pool_map[e, gi])
            wmap = wmap.astype(BF16)
            scale = pool_scale[e].reshape(1, d_pool)
            xp, pool_p, conv_p = _even_prompt(xp, batch, seq, g_mix, win_even, wmap, scale, w_sconv[e],
                                              wout_even, e)
            pool_prev = state_pool[e].transpose(1, 0, 2).reshape(POOL_BUF * dec_batch, d_pool)
            conv_prev = state_sconv[e].transpose(1, 0, 2).reshape(2 * dec_batch, d_pool)
            xs, pool_s, conv_s = _even_sample(xs, dec_batch, g_mix, win_even, wmap, scale, w_sconv[e],
                                              wout_even, pool_prev, conv_prev, e)
            outs["pool_p"].append(pool_p)
            outs["conv_p"].append(conv_p)
            outs["pool_s"].append(pool_s.reshape(POOL_BUF, dec_batch, d_pool).transpose(1, 0, 2))
            outs["conv_s"].append(conv_s.reshape(2, dec_batch, d_pool).transpose(1, 0, 2))
        else:
            o = i // 2
            w_kr = w_in_odd[o][:, q_lora + kv_lora:]
            win = jnp.concatenate([w_in_odd[o], _swap_halves(w_kr)], axis=1).astype(BF16)
            wq_nope = w_uq[o][:, :, :qk_nope].reshape(q_lora, heads * qk_nope)
            wq_rope = w_uq[o][:, :, qk_nope:]
            wq = jnp.concatenate([wq_nope, wq_rope.reshape(q_lora, heads * rope),
                                  _swap_halves(wq_rope).reshape(q_lora, heads * rope)], axis=1).astype(BF16)
            qn = q_norm[o].reshape(1, q_lora)
            kvn = kv_norm[o].reshape(1, kv_lora)
            wk = w_uk[o].reshape(kv_lora, heads * qk_nope).astype(BF16)
            wvt = w_uv[o].transpose(1, 2, 0).reshape(heads * v_head, kv_lora).astype(BF16)
            ckv_p, kr_p, q, k, vt = _mla_proj_prompt(xp, batch, seq, g_mix, win, qn, kvn, wq, wk, wvt,
                                                     cos_p, sin_p, heads=heads, rope=rope,
                                                     attn_scale=attn_scale)
            o_p = _attn_prompt(q, k, vt)
            xp = _attn_out_prompt(o_p.reshape(batch * seq, heads * v_head), xp, g_mix, wout_odd, o)
            outs["ckv_p"].append(ckv_p.reshape(batch, seq, kv_lora))
            outs["kr_p"].append(kr_p.reshape(batch, seq, rope))
            wukt = w_uk[o].transpose(1, 2, 0).astype(BF16)
            wuv = w_uv[o].transpose(1, 0, 2).astype(BF16)
            ckv_s, kr_s, qlat, qrot = _mla_proj_sample(xs, g_mix, win, qn, kvn, wq, wukt, cos_s, sin_s,
                                                       heads=heads, rope=rope, attn_scale=attn_scale)
            ckv_s = ckv_s.reshape(n_new, dec_batch, kv_lora).transpose(1, 0, 2)
            kr_s = kr_s.reshape(n_new, dec_batch, rope).transpose(1, 0, 2)
            qlat = qlat.reshape(heads, n_new, dec_batch, kv_lora).transpose(2, 0, 1, 3).reshape(
                dec_batch, heads * n_new, kv_lora)
            qrot = qrot.reshape(n_new, dec_batch, heads, rope).transpose(1, 2, 0, 3).reshape(
                dec_batch, heads * n_new, rope)
            o_lat = _attn_decode(page_flat, qlat, qrot, ckv_s, kr_s, cache_ckv, cache_krt, o, n_pages)
            o_lat = o_lat.reshape(dec_batch, heads, n_new, kv_lora).transpose(1, 2, 0, 3).reshape(
                heads, n_new * dec_batch, kv_lora)
            xs = _attn_out_sample(o_lat, xs, g_mix, wuv, wout_odd, o)
            outs["ckv_s"].append(ckv_s)
            outs["kr_s"].append(kr_s)
        xp, ffn_p = _ffn_prompt(xp, batch, seq, g_ffn, wup_all, w_ffn_conv[i], wdown_all, i)
        ffn_prev = state_ffn[i].transpose(1, 0, 2).reshape(2 * dec_batch, d_ff)
        xs, ffn_s = _ffn_sample(xs, dec_batch, g_ffn, wup_all, w_ffn_conv[i], wdown_all, ffn_prev, i)
        outs["ffn_p"].append(ffn_p)
        outs["ffn_s"].append(ffn_s.reshape(2, dec_batch, d_ff).transpose(1, 0, 2))

    y_prompt = xp.reshape(batch, seq, d_model)
    y_sample = xs.reshape(n_new, dec_batch, d_model).transpose(1, 0, 2)
    st = {k: jnp.stack(v) for k, v in outs.items()}
    return (y_prompt, y_sample, st["pool_p"], st["pool_s"], st["conv_p"], st["conv_s"],
            st["ckv_p"], st["kr_p"], st["ckv_s"], st["kr_s"], st["ffn_p"], st["ffn_s"])
```
